```python
import math
import jax, jax.numpy as jnp
from jax import lax
import numpy as np

D_MODEL = 2048
BATCH = 2
SEQ = 8192
DEPTH = 2

N_MEM = 256
MEM_HEADS = 4
MEM_HEAD_DIM = D_MODEL // MEM_HEADS
CONV_WIDTH = D_MODEL // 4
CONV_GROUPS = 4
CONV_K = 3
GMLP_WIDTH = D_MODEL // 4
GMLP_GROUPS = 4
GMLP_GROUP_DIM = GMLP_WIDTH // GMLP_GROUPS
CHUNK = 128
DIFF_HEAD_DIM = 128
DIFF_WIDTH = D_MODEL // 2
DIFF_HEADS = DIFF_WIDTH // (2 * DIFF_HEAD_DIM)
Q_BLOCK = 128
MIX_WIDTH = CONV_WIDTH + GMLP_WIDTH + DIFF_WIDTH
SPLITS = (CONV_WIDTH, 2 * CONV_WIDTH, 3 * CONV_WIDTH,
          3 * CONV_WIDTH + GMLP_WIDTH, 3 * CONV_WIDTH + 2 * GMLP_WIDTH,
          3 * CONV_WIDTH + 2 * GMLP_WIDTH + DIFF_WIDTH,
          3 * CONV_WIDTH + 2 * GMLP_WIDTH + 2 * DIFF_WIDTH)
IN_COLS = 3 * CONV_WIDTH + 2 * GMLP_WIDTH + 3 * DIFF_WIDTH
NUM_BUCKETS = 32
MAX_DISTANCE = 128
BIAS_HEADS = 2 * DIFF_HEADS
D_FF = -(-8 * D_MODEL // (3 * 256)) * 256
ALPHA = (2 * DEPTH) ** 0.25
BETA = (8 * DEPTH) ** -0.25
LN_EPS = 1e-5

kernel_name = "hybrid_conv_gmlp_diffattn_deepnorm"


def layer_norm(x, g, b):
    xf = x.astype(jnp.float32)
    mu = jnp.mean(xf, axis=-1, keepdims=True)
    var = jnp.mean(jnp.square(xf - mu), axis=-1, keepdims=True)
    return ((xf - mu) * lax.rsqrt(var + LN_EPS) * g + b).astype(x.dtype)


def rms_norm(x, g):
    xf = x.astype(jnp.float32)
    return xf * lax.rsqrt(jnp.mean(xf * xf, axis=-1, keepdims=True) + LN_EPS) * g


def causal_bucket(n):
    max_exact = NUM_BUCKETS // 2
    nf = jnp.maximum(n, 1).astype(jnp.float32)
    large = max_exact + (jnp.log(nf / max_exact) / math.log(MAX_DISTANCE / max_exact)
                         * (NUM_BUCKETS - max_exact)).astype(jnp.int32)
    large = jnp.minimum(large, NUM_BUCKETS - 1)
    return jnp.where(n < max_exact, n, large)


def short_conv_mixer(b_gate, c_gate, h, conv_w):
    z = c_gate * h
    y = lax.conv_general_dilated(
        z, conv_w[:, None, :].astype(z.dtype), window_strides=(1,),
        padding=[(CONV_K - 1, 0)], dimension_numbers=('NWC', 'WIO', 'NWC'),
        feature_group_count=CONV_WIDTH)
    return b_gate * y


def chunk_gmlp_mixer(u, v, ln_g, ln_b, w_s, b_s):
    bsz, seq, _ = v.shape
    u = jax.nn.gelu(u)
    v = layer_norm(jax.nn.gelu(v), ln_g, ln_b)
    vc = v.reshape(bsz, seq // CHUNK, CHUNK, GMLP_GROUPS, GMLP_GROUP_DIM)
    causal = jnp.tril(jnp.ones((CHUNK, CHUNK), dtype=bool))
    ws = jnp.where(causal, w_s, jnp.zeros_like(w_s))
    sv = jnp.einsum('gts,bnsgc->bntgc', ws, vc) + jnp.transpose(b_s)[None, None, :, :, None]
    return u * sv.reshape(bsz, seq, GMLP_WIDTH)


def diff_attention(q, k, v, lam, lam_init, subln_g, bias_by_dist):
    bsz, seq = q.shape[0], q.shape[1]
    nb = seq // Q_BLOCK
    scale = DIFF_HEAD_DIM ** -0.5
    qb = jnp.moveaxis(q.reshape(bsz, nb, Q_BLOCK, DIFF_HEADS, 2, DIFF_HEAD_DIM), 1, 0)
    k_pos = jnp.arange(seq)

    def block(args):
        q_blk, i = args
        q_pos = i * Q_BLOCK + jnp.arange(Q_BLOCK)
        dist = q_pos[:, None] - k_pos[None, :]
        bias = jnp.transpose(bias_by_dist[jnp.maximum(dist, 0)], (2, 3, 0, 1))
        logits = jnp.einsum('bqhmd,bkhmd->bhmqk', q_blk, k).astype(jnp.float32) * scale + bias
        logits = jnp.where(dist >= 0, logits, -jnp.inf)
        p = jax.nn.softmax(logits, axis=-1)
        a = p[:, :, 0] - lam * p[:, :, 1]
        return jnp.einsum('bhqk,bkhe->bqhe', a, v)

    out = lax.map(block, (qb, jnp.arange(nb)))
    out = jnp.moveaxis(out, 0, 1).reshape(bsz, seq, DIFF_HEADS, 2 * DIFF_HEAD_DIM)
    out = rms_norm(out, subln_g) * (1.0 - lam_init)
    return out.reshape(bsz, seq, DIFF_WIDTH).astype(v.dtype)


def hybrid_mixer(x, lam_init, bias_by_dist, w_in, conv_w, gmlp_ln_g, gmlp_ln_b,
                 gmlp_ws, gmlp_bs, lambda_q, lambda_k, subln_g, w_out):
    bsz, seq, _ = x.shape
    proj = x @ w_in
    b_gate, c_gate, h, u, v, q, k, va = jnp.split(proj, SPLITS, axis=-1)
    y_a = short_conv_mixer(b_gate, c_gate, h, conv_w)
    y_b = chunk_gmlp_mixer(u, v, gmlp_ln_g, gmlp_ln_b, gmlp_ws, gmlp_bs)
    lq = lambda_q.astype(jnp.float32)
    lk = lambda_k.astype(jnp.float32)
    lam = jnp.exp(jnp.sum(lq[0] * lk[0])) - jnp.exp(jnp.sum(lq[1] * lk[1])) + lam_init
    y_c = diff_attention(q.reshape(bsz, seq, DIFF_HEADS, 2, DIFF_HEAD_DIM),
                         k.reshape(bsz, seq, DIFF_HEADS, 2, DIFF_HEAD_DIM),
                         va.reshape(bsz, seq, DIFF_HEADS, 2 * DIFF_HEAD_DIM),
                         lam, lam_init, subln_g, bias_by_dist)
    y = jnp.concatenate([y_a, y_b.astype(y_a.dtype), y_c.astype(y_a.dtype)], axis=-1)
    return y @ w_out


def memory_attention(x, mem, w_q, w_kv, w_o):
    bsz, seq, _ = x.shape
    n_mem = mem.shape[1]
    q = (x @ w_q).reshape(bsz, seq, MEM_HEADS, MEM_HEAD_DIM)
    kv = (mem @ w_kv).reshape(bsz, n_mem, 2, MEM_HEADS, MEM_HEAD_DIM)
    k, v = kv[:, :, 0], kv[:, :, 1]
    logits = jnp.einsum('bshd,bmhd->bhsm', q, k).astype(jnp.float32) * MEM_HEAD_DIM ** -0.5
    p = jax.nn.softmax(logits, axis=-1).astype(x.dtype)
    o = jnp.einsum('bhsm,bmhd->bshd', p, v).reshape(bsz, seq, MEM_HEADS * MEM_HEAD_DIM)
    return o @ w_o


def swiglu(x, w_gu, w_down):
    g, u = jnp.split(x @ w_gu, 2, axis=-1)
    return (jax.nn.silu(g) * u) @ w_down


def setup_inputs(seed: int = 0) -> dict:
    key = jax.random.key(seed)
    ks = jax.random.split(key, 32)
    f32 = jnp.float32

    def nrm(k, shape, scale):
        return jax.random.normal(k, shape, f32) * scale

    def gain(k, shape):
        return 1.0 + 0.02 * jax.random.normal(k, shape, f32)

    L = DEPTH
    return {
        "x": nrm(ks[0], (BATCH, SEQ, D_MODEL), 1.0),
        "mem": nrm(ks[1], (BATCH, N_MEM, D_MODEL), 1.0),
        "rel_bias": nrm(ks[2], (NUM_BUCKETS, BIAS_HEADS), 0.5),
        "w_in": nrm(ks[3], (L, D_MODEL, IN_COLS), D_MODEL ** -0.5),
        "conv_w": nrm(ks[4], (L, CONV_K, CONV_WIDTH), CONV_K ** -0.5),
        "gmlp_ln_g": gain(ks[5], (L, GMLP_WIDTH)),
        "gmlp_ln_b": nrm(ks[6], (L, GMLP_WIDTH), 0.02),
        "gmlp_ws": nrm(ks[7], (L, GMLP_GROUPS, CHUNK, CHUNK), CHUNK ** -0.5),
        "gmlp_bs": 1.0 + nrm(ks[8], (L, GMLP_GROUPS, CHUNK), 0.05),
        "diff_lambda_q": nrm(ks[9], (L, 2, DIFF_HEAD_DIM), 0.1),
        "diff_lambda_k": nrm(ks[10], (L, 2, DIFF_HEAD_DIM), 0.1),
        "diff_subln_g": gain(ks[11], (L, 2 * DIFF_HEAD_DIM)),
        "w_mix_out": nrm(ks[12], (L, MIX_WIDTH, D_MODEL), BETA * MIX_WIDTH ** -0.5),
        "ln_mix_g": gain(ks[13], (L, D_MODEL)),
        "ln_mix_b": nrm(ks[14], (L, D_MODEL), 0.02),
        "w_mem_q": nrm(ks[15], (L, D_MODEL, D_MODEL), D_MODEL ** -0.5),
        "w_mem_kv": nrm(ks[16], (L, D_MODEL, 2 * D_MODEL), D_MODEL ** -0.5),
        "w_mem_out": nrm(ks[17], (L, D_MODEL, D_MODEL), BETA * D_MODEL ** -0.5),
        "ln_mem_g": gain(ks[18], (L, D_MODEL)),
        "ln_mem_b": nrm(ks[19], (L, D_MODEL), 0.02),
        "w_ffn_gu": nrm(ks[20], (L, D_MODEL, 2 * D_FF), D_MODEL ** -0.5),
        "w_ffn_down": nrm(ks[21], (L, D_FF, D_MODEL), BETA * D_FF ** -0.5),
        "ln_ffn_g": gain(ks[22], (L, D_MODEL)),
        "ln_ffn_b": nrm(ks[23], (L, D_MODEL), 0.02),
    }


def reference(x, mem, rel_bias, w_in, conv_w, gmlp_ln_g, gmlp_ln_b, gmlp_ws, gmlp_bs,
              diff_lambda_q, diff_lambda_k, diff_subln_g, w_mix_out, ln_mix_g, ln_mix_b,
              w_mem_q, w_mem_kv, w_mem_out, ln_mem_g, ln_mem_b,
              w_ffn_gu, w_ffn_down, ln_ffn_g, ln_ffn_b):
    seq = x.shape[1]
    bias_by_dist = rel_bias.astype(jnp.float32)[causal_bucket(jnp.arange(seq))]
    bias_by_dist = bias_by_dist.reshape(seq, DIFF_HEADS, 2)
    for l in range(DEPTH):
        lam_init = 0.8 - 0.6 * math.exp(-0.3 * l)
        mix = hybrid_mixer(x, lam_init, bias_by_dist, w_in[l], conv_w[l], gmlp_ln_g[l],
                           gmlp_ln_b[l], gmlp_ws[l], gmlp_bs[l], diff_lambda_q[l],
                           diff_lambda_k[l], diff_subln_g[l], w_mix_out[l])
        x = layer_norm(ALPHA * x + mix, ln_mix_g[l], ln_mix_b[l])
        x = layer_norm(ALPHA * x + memory_attention(x, mem, w_mem_q[l], w_mem_kv[l], w_mem_out[l]),
                       ln_mem_g[l], ln_mem_b[l])
        x = layer_norm(ALPHA * x + swiglu(x, w_ffn_gu[l], w_ffn_down[l]),
                       ln_ffn_g[l], ln_ffn_b[l])
    return x
```

```python
import functools
import math

import jax
import jax.numpy as jnp
from jax import lax
from jax.experimental import pallas as pl
from jax.experimental.pallas import tpu as pltpu

D_MODEL = 2048
DEPTH = 2
N_MEM = 256
MEM_HEADS = 4
MEM_HEAD_DIM = D_MODEL // MEM_HEADS
CONV_WIDTH = D_MODEL // 4
CONV_K = 3
GMLP_WIDTH = D_MODEL // 4
GMLP_GROUPS = 4
GMLP_GROUP_DIM = GMLP_WIDTH // GMLP_GROUPS
CHUNK = 128
DIFF_HEAD_DIM = 128
DIFF_WIDTH = D_MODEL // 2
DIFF_HEADS = DIFF_WIDTH // (2 * DIFF_HEAD_DIM)
AB_COLS = 3 * CONV_WIDTH + 2 * GMLP_WIDTH
QKV_COLS = 3 * DIFF_WIDTH
NUM_BUCKETS = 32
MAX_DISTANCE = 128
D_FF = -(-8 * D_MODEL // (3 * 256)) * 256
ALPHA = (2 * DEPTH) ** 0.25
LN_EPS = 1e-5

V7X_VMEM_BYTES = 64 * 1024 * 1024
V7X_LANES = 128
V7X_SUBLANES = 8
VMEM_LIMIT_BYTES = V7X_VMEM_BYTES - 8 * 1024 * 1024

MASK_VALUE = -1e30
LOG2E = math.log2(math.e)

F32 = jnp.float32
BF16 = jnp.bfloat16


def _params(*semantics):
    return pltpu.CompilerParams(dimension_semantics=semantics,
                                vmem_limit_bytes=VMEM_LIMIT_BYTES)


def _layer_norm_rows(x, g, b):
    mu = jnp.mean(x, axis=-1, keepdims=True)
    xc = x - mu
    var = jnp.mean(xc * xc, axis=-1, keepdims=True)
    return xc * lax.rsqrt(var + LN_EPS) * g + b


def _mm_kernel(a_ref, w_ref, o_ref):
    o_ref[...] = jnp.dot(a_ref[...], w_ref[...],
                         preferred_element_type=F32).astype(o_ref.dtype)


def _matmul(a, w, out_dtype, tm, tn):
    m, k = a.shape
    n = w.shape[1]
    return pl.pallas_call(
        _mm_kernel,
        grid=(n // tn, m // tm),
        in_specs=[pl.BlockSpec((tm, k), lambda j, i: (i, 0)),
                  pl.BlockSpec((k, tn), lambda j, i: (0, j))],
        out_specs=pl.BlockSpec((tm, tn), lambda j, i: (i, j)),
        out_shape=jax.ShapeDtypeStruct((m, n), out_dtype),
        compiler_params=_params("parallel", "parallel"),
        name="matmul",
    )(a, w)


def _mm_res_ln_kernel(*refs, n_a):
    a_refs = refs[:n_a]
    w_ref, res_ref, g_ref, b_ref, o_ref, ob_ref = refs[n_a:]
    acc = None
    row = 0
    for a_ref in a_refs:
        k = a_ref.shape[1]
        part = jnp.dot(a_ref[...], w_ref[row:row + k, :], preferred_element_type=F32)
        acc = part if acc is None else acc + part
        row += k
    y = _layer_norm_rows(ALPHA * res_ref[...] + acc, g_ref[...], b_ref[...])
    o_ref[...] = y
    ob_ref[...] = y.astype(BF16)


def _matmul_res_ln(a_list, w, res, g, b, tm):
    m = res.shape[0]
    k, n = w.shape
    in_specs = [pl.BlockSpec((tm, a.shape[1]), lambda i: (i, 0)) for a in a_list]
    in_specs += [pl.BlockSpec((k, n), lambda i: (0, 0)),
                 pl.BlockSpec((tm, n), lambda i: (i, 0)),
                 pl.BlockSpec((1, n), lambda i: (0, 0)),
                 pl.BlockSpec((1, n), lambda i: (0, 0))]
    return pl.pallas_call(
        functools.partial(_mm_res_ln_kernel, n_a=len(a_list)),
        grid=(m // tm,),
        in_specs=in_specs,
        out_specs=[pl.BlockSpec((tm, n), lambda i: (i, 0)),
                   pl.BlockSpec((tm, n), lambda i: (i, 0))],
        out_shape=[jax.ShapeDtypeStruct((m, n), F32),
                   jax.ShapeDtypeStruct((m, n), BF16)],
        compiler_params=_params("parallel"),
        name="matmul_res_ln",
    )(*a_list, w, res, g.reshape(1, n), b.reshape(1, n))


def _mixer_ab_kernel(bg_ref, cg_ref, h_ref, u_ref, v_ref, cgp_ref, hp_ref,
                     cw_ref, lg_ref, lb_ref, ws_ref, bst_ref, o_ref, *, tm, tiles_per_seq):
    i = pl.program_id(0)
    z = cg_ref[...] * h_ref[...]
    zp = cgp_ref[...] * hp_ref[...]
    zp = jnp.where(i % tiles_per_seq == 0, 0.0, zp)
    rows = lax.broadcasted_iota(jnp.int32, (tm, 1), 0)
    last = zp[V7X_SUBLANES - 1:V7X_SUBLANES, :]
    last2 = zp[V7X_SUBLANES - 2:V7X_SUBLANES - 1, :]
    z1 = jnp.where(rows == 0, last, pltpu.roll(z, 1, axis=0))
    z2 = jnp.where(rows == 0, last2, jnp.where(rows == 1, last, pltpu.roll(z, 2, axis=0)))
    cw = cw_ref[...]
    conv = cw[0:1, :] * z2 + cw[1:2, :] * z1 + cw[2:3, :] * z
    o_ref[:, 0:CONV_WIDTH] = (bg_ref[...] * conv).astype(o_ref.dtype)

    u = jax.nn.gelu(u_ref[...])
    vn = _layer_norm_rows(jax.nn.gelu(v_ref[...]), lg_ref[...], lb_ref[...]).astype(BF16)
    tr = lax.broadcasted_iota(jnp.int32, (CHUNK, CHUNK), 0)
    tc = lax.broadcasted_iota(jnp.int32, (CHUNK, CHUNK), 1)
    causal = tr >= tc
    bst = bst_ref[...]
    for g in range(GMLP_GROUPS):
        wg = jnp.where(causal, ws_ref[g], 0.0).astype(BF16)
        bias = bst[:, g:g + 1]
        c0 = g * GMLP_GROUP_DIM
        for c in range(tm // CHUNK):
            r0 = c * CHUNK
            sv = jnp.dot(wg, vn[r0:r0 + CHUNK, c0:c0 + GMLP_GROUP_DIM],
                         preferred_element_type=F32) + bias
            o_ref[r0:r0 + CHUNK, CONV_WIDTH + c0:CONV_WIDTH + c0 + GMLP_GROUP_DIM] = (
                u[r0:r0 + CHUNK, c0:c0 + GMLP_GROUP_DIM] * sv).astype(o_ref.dtype)


def _mixer_ab(proj_ab, conv_w, ln_g, ln_b, ws, bs, seq, tm):
    m = proj_ab.shape[0]
    w = CONV_WIDTH
    halo_blocks = tm // V7X_SUBLANES

    def col(c):
        return pl.BlockSpec((tm, w), lambda i, c=c: (i, c))

    def halo(c):
        return pl.BlockSpec((V7X_SUBLANES, w),
                            lambda i, c=c: (jnp.maximum(i * halo_blocks - 1, 0), c))

    def whole(shape):
        return pl.BlockSpec(shape, lambda i: (0,) * len(shape))

    return pl.pallas_call(
        functools.partial(_mixer_ab_kernel, tm=tm, tiles_per_seq=seq // tm),
        grid=(m // tm,),
        in_specs=[col(0), col(1), col(2), col(3), col(4), halo(1), halo(2),
                  whole((CONV_K, w)), whole((1, w)), whole((1, w)),
                  whole((GMLP_GROUPS, CHUNK, CHUNK)), whole((CHUNK, GMLP_GROUPS))],
        out_specs=pl.BlockSpec((tm, 2 * w), lambda i: (i, 0)),
        out_shape=jax.ShapeDtypeStruct((m, 2 * w), BF16),
        compiler_params=_params("parallel"),
        name="mixer_ab",
    )(proj_ab, proj_ab, proj_ab, proj_ab, proj_ab, proj_ab, proj_ab,
      conv_w, ln_g.reshape(1, w), ln_b.reshape(1, w), ws, jnp.transpose(bs))


def _causal_bucket(n):
    max_exact = NUM_BUCKETS // 2
    nf = jnp.maximum(n, 1).astype(F32)
    large = max_exact + (jnp.log(nf / max_exact) / math.log(MAX_DISTANCE / max_exact)
                         * (NUM_BUCKETS - max_exact)).astype(jnp.int32)
    large = jnp.minimum(large, NUM_BUCKETS - 1)
    return jnp.where(n < max_exact, n, large)


def _near_bias_tiles(rel_bias, tq, tk):
    r = tq // tk
    qpos = jnp.arange(tq)[:, None]
    kpos = jnp.arange(tk)[None, :]
    tiles = []
    for e in range(-1, r):
        dist = qpos - kpos - e * tk
        rel = rel_bias.astype(F32)[_causal_bucket(jnp.maximum(dist, 0))]
        rel = (rel - rel_bias.astype(F32)[NUM_BUCKETS - 1]) * LOG2E
        rel = jnp.where((dist >= 0)[:, :, None], rel, MASK_VALUE)
        tiles.append(rel)
    t = jnp.stack(tiles, axis=0)
    t = t.reshape(r + 1, tq, tk, DIFF_HEADS, 2)
    return jnp.transpose(t, (3, 4, 0, 1, 2))


def _diff_attn_kernel(q_ref, k_ref, v_ref, bias_ref, lq_ref, lk_ref, g_ref, o_ref, acc_ref,
                      *, lam_init, tq, tk):
    qi = pl.program_id(2)
    r = tq // tk
    dh = DIFF_HEAD_DIM
    first_near = qi * r - 1
    n_far = jnp.maximum(first_near, 0)
    logit_scale = dh ** -0.5 * LOG2E
    outs = []
    for mp in range(2):
        q = q_ref[:, mp * dh:(mp + 1) * dh]
        acc_ref[...] = jnp.zeros_like(acc_ref)

        def block(j, carry, bias):
            m_i, l_i = carry
            start = pl.multiple_of(j * tk, tk)
            k = k_ref[pl.ds(start, tk), mp * dh:(mp + 1) * dh]
            s = lax.dot_general(q, k, (((1,), (1,)), ((), ())), preferred_element_type=F32)
            t = s * logit_scale
            if bias is not None:
                t = t + bias
            m_new = jnp.maximum(m_i, jnp.max(t, axis=-1, keepdims=True))
            p = jnp.exp2(t - m_new)
            a = jnp.exp2(m_i - m_new)
            l_new = a * l_i + jnp.sum(p, axis=-1, keepdims=True)
            pv = jnp.dot(p.astype(BF16), v_ref[pl.ds(start, tk), :], preferred_element_type=F32)
            acc_ref[...] = a * acc_ref[...] + pv
            return m_new, l_new

        carry = (jnp.full((tq, 1), MASK_VALUE, F32), jnp.zeros((tq, 1), F32))
        carry = lax.fori_loop(0, n_far, lambda j, c: block(j, c, None), carry)
        carry = lax.fori_loop(n_far, qi * r,
                              lambda j, c: block(j, c, bias_ref[mp, 0]), carry)
        for e in range(r):
            carry = block(qi * r + e, carry, bias_ref[mp, e + 1])
        outs.append(acc_ref[...] / carry[1])

    sums = jnp.sum(lq_ref[...] * lk_ref[...], axis=-1, keepdims=True)
    ex = jnp.exp(sums)
    lam = ex[0:1, :] - ex[1:2, :] + lam_init
    o = outs[0] - lam * outs[1]
    o = o * lax.rsqrt(jnp.mean(o * o, axis=-1, keepdims=True) + LN_EPS) * g_ref[...]
    o_ref[...] = (o * (1.0 - lam_init)).astype(o_ref.dtype)


def _diff_attention(qkv, bias_tiles, lam_q, lam_k, subln_g, lam_init, bsz, seq, tq, tk):
    m = qkv.shape[0]
    hd = 2 * DIFF_HEAD_DIM
    nq = seq // tq
    r = tq // tk
    return pl.pallas_call(
        functools.partial(_diff_attn_kernel, lam_init=lam_init, tq=tq, tk=tk),
        grid=(bsz, DIFF_HEADS, nq),
        in_specs=[
            pl.BlockSpec((tq, hd), lambda b, h, i: (b * nq + i, h)),
            pl.BlockSpec((seq, hd), lambda b, h, i: (b, DIFF_HEADS + h)),
            pl.BlockSpec((seq, hd), lambda b, h, i: (b, 2 * DIFF_HEADS + h)),
            pl.BlockSpec((None, 2, r + 1, tq, tk), lambda b, h, i: (h, 0, 0, 0, 0)),
            pl.BlockSpec((2, DIFF_HEAD_DIM), lambda b, h, i: (0, 0)),
            pl.BlockSpec((2, DIFF_HEAD_DIM), lambda b, h, i: (0, 0)),
            pl.BlockSpec((1, hd), lambda b, h, i: (0, 0)),
        ],
        out_specs=pl.BlockSpec((tq, hd), lambda b, h, i: (b * nq + i, h)),
        out_shape=jax.ShapeDtypeStruct((m, DIFF_WIDTH), BF16),
        scratch_shapes=[pltpu.VMEM((tq, hd), F32)],
        compiler_params=_params("parallel", "parallel", "arbitrary"),
        name="diff_attention",
    )(qkv, qkv, qkv, bias_tiles, lam_q, lam_k, subln_g.reshape(1, hd))


def _mem_attn_kernel(q_ref, kv_ref, o_ref):
    d = MEM_HEAD_DIM
    scale = d ** -0.5
    for h in range(MEM_HEADS):
        q = q_ref[:, h * d:(h + 1) * d]
        k = kv_ref[:, h * d:(h + 1) * d]
        v = kv_ref[:, D_MODEL + h * d:D_MODEL + (h + 1) * d]
        s = lax.dot_general(q, k, (((1,), (1,)), ((), ())), preferred_element_type=F32) * scale
        s = s - jnp.max(s, axis=-1, keepdims=True)
        p = jnp.exp(s)
        p = p / jnp.sum(p, axis=-1, keepdims=True)
        o_ref[:, h * d:(h + 1) * d] = jnp.dot(p.astype(BF16), v,
                                              preferred_element_type=F32).astype(o_ref.dtype)


def _mem_attention(q, kv, seq, tm):
    m = q.shape[0]
    tiles_per_seq = seq // tm
    return pl.pallas_call(
        _mem_attn_kernel,
        grid=(m // tm,),
        in_specs=[pl.BlockSpec((tm, D_MODEL), lambda i: (i, 0)),
                  pl.BlockSpec((N_MEM, 2 * D_MODEL), lambda i: (i // tiles_per_seq, 0))],
        out_specs=pl.BlockSpec((tm, D_MODEL), lambda i: (i, 0)),
        out_shape=jax.ShapeDtypeStruct((m, D_MODEL), BF16),
        compiler_params=_params("parallel"),
        name="mem_attention",
    )(q, kv)


def _ffn_kernel(xb_ref, wg_ref, wu_ref, wd_ref, res_ref, g_ref, b_ref, o_ref, ob_ref, acc_ref):
    f = pl.program_id(1)
    xb = xb_ref[...]
    gate = jnp.dot(xb, wg_ref[...], preferred_element_type=F32)
    up = jnp.dot(xb, wu_ref[...], preferred_element_type=F32)
    hid = (gate * (1.0 / (1.0 + jnp.exp(-gate))) * up).astype(BF16)
    part = jnp.dot(hid, wd_ref[...], preferred_element_type=F32)

    @pl.when(f == 0)
    def _():
        acc_ref[...] = part

    @pl.when(f > 0)
    def _():
        acc_ref[...] += part

    @pl.when(f == pl.num_programs(1) - 1)
    def _():
        y = _layer_norm_rows(ALPHA * res_ref[...] + acc_ref[...], g_ref[...], b_ref[...])
        o_ref[...] = y
        ob_ref[...] = y.astype(BF16)


def _ffn(xb, w_gu, w_down, res, g, b, tm, tf):
    m, d = xb.shape
    nf = D_FF // tf
    return pl.pallas_call(
        _ffn_kernel,
        grid=(m // tm, nf),
        in_specs=[pl.BlockSpec((tm, d), lambda i, f: (i, 0)),
                  pl.BlockSpec((d, tf), lambda i, f: (0, f)),
                  pl.BlockSpec((d, tf), lambda i, f: (0, nf + f)),
                  pl.BlockSpec((tf, d), lambda i, f: (f, 0)),
                  pl.BlockSpec((tm, d), lambda i, f: (i, 0)),
                  pl.BlockSpec((1, d), lambda i, f: (0, 0)),
                  pl.BlockSpec((1, d), lambda i, f: (0, 0))],
        out_specs=[pl.BlockSpec((tm, d), lambda i, f: (i, 0)),
                   pl.BlockSpec((tm, d), lambda i, f: (i, 0))],
        out_shape=[jax.ShapeDtypeStruct((m, d), F32),
                   jax.ShapeDtypeStruct((m, d), BF16)],
        scratch_shapes=[pltpu.VMEM((tm, d), F32)],
        compiler_params=_params("parallel", "arbitrary"),
        name="ffn",
    )(xb, w_gu, w_gu, w_down, res, g.reshape(1, d), b.reshape(1, d))


ATTN_TQ = 256
ATTN_TK = 256


def kernel(x, mem, rel_bias, w_in, conv_w, gmlp_ln_g, gmlp_ln_b, gmlp_ws, gmlp_bs,
           diff_lambda_q, diff_lambda_k, diff_subln_g, w_mix_out, ln_mix_g, ln_mix_b,
           w_mem_q, w_mem_kv, w_mem_out, ln_mem_g, ln_mem_b,
           w_ffn_gu, w_ffn_down, ln_ffn_g, ln_ffn_b):
    bsz, seq, d = x.shape
    m = bsz * seq
    x = x.reshape(m, d)
    xb = x.astype(BF16)
    memb = mem.reshape(bsz * N_MEM, d).astype(BF16)
    bias_tiles = _near_bias_tiles(rel_bias, ATTN_TQ, ATTN_TK)
    for l in range(DEPTH):
        lam_init = 0.8 - 0.6 * math.exp(-0.3 * l)
        w_in_b = w_in[l].astype(BF16)
        proj_ab = _matmul(xb, w_in_b[:, :AB_COLS], F32, tm=1024, tn=512)
        qkv = _matmul(xb, w_in_b[:, AB_COLS:], BF16, tm=1024, tn=1024)
        y_ab = _mixer_ab(proj_ab, conv_w[l], gmlp_ln_g[l], gmlp_ln_b[l], gmlp_ws[l],
                         gmlp_bs[l], seq, tm=256)
        y_c = _diff_attention(qkv, bias_tiles, diff_lambda_q[l], diff_lambda_k[l],
                              diff_subln_g[l], lam_init, bsz, seq, ATTN_TQ, ATTN_TK)
        x, xb = _matmul_res_ln([y_ab, y_c], w_mix_out[l].astype(BF16), x,
                               ln_mix_g[l], ln_mix_b[l], tm=256)

        q_mem = _matmul(xb, w_mem_q[l].astype(BF16), BF16, tm=1024, tn=1024)
        kv_mem = _matmul(memb, w_mem_kv[l].astype(BF16), BF16, tm=bsz * N_MEM, tn=1024)
        o_mem = _mem_attention(q_mem, kv_mem, seq, tm=512)
        x, xb = _matmul_res_ln([o_mem], w_mem_out[l].astype(BF16), x,
                               ln_mem_g[l], ln_mem_b[l], tm=256)

        x, xb = _ffn(xb, w_ffn_gu[l].astype(BF16), w_ffn_down[l].astype(BF16), x,
                     ln_ffn_g[l], ln_ffn_b[l], tm=512, tf=512)
    return x.reshape(bsz, seq, d)
```

```python
import functools
import math

import jax
import jax.numpy as jnp
from jax import lax
from jax.experimental import pallas as pl
from jax.experimental.pallas import tpu as pltpu

D_MODEL = 2048
DEPTH = 2
N_MEM = 256
MEM_HEADS = 4
MEM_HEAD_DIM = D_MODEL // MEM_HEADS
CONV_WIDTH = D_MODEL // 4
CONV_K = 3
GMLP_WIDTH = D_MODEL // 4
GMLP_GROUPS = 4
GMLP_GROUP_DIM = GMLP_WIDTH // GMLP_GROUPS
CHUNK = 128
DIFF_HEAD_DIM = 128
DIFF_WIDTH = D_MODEL // 2
DIFF_HEADS = DIFF_WIDTH // (2 * DIFF_HEAD_DIM)
AB_COLS = 3 * CONV_WIDTH + 2 * GMLP_WIDTH
QKV_COLS = 3 * DIFF_WIDTH
NUM_BUCKETS = 32
MAX_DISTANCE = 128
D_FF = -(-8 * D_MODEL // (3 * 256)) * 256
ALPHA = (2 * DEPTH) ** 0.25
LN_EPS = 1e-5

V7X_VMEM_BYTES = 64 * 1024 * 1024
V7X_LANES = 128
V7X_SUBLANES = 8
VMEM_LIMIT_BYTES = V7X_VMEM_BYTES - 8 * 1024 * 1024

MASK_VALUE = -1e30
LOG2E = math.log2(math.e)

F32 = jnp.float32
BF16 = jnp.bfloat16


def _params(*semantics):
    return pltpu.CompilerParams(dimension_semantics=semantics,
                                vmem_limit_bytes=VMEM_LIMIT_BYTES)


def _layer_norm_rows(x, g, b):
    mu = jnp.mean(x, axis=-1, keepdims=True)
    xc = x - mu
    var = jnp.mean(xc * xc, axis=-1, keepdims=True)
    return xc * lax.rsqrt(var + LN_EPS) * g + b


def _mm_kernel(a_ref, w_ref, o_ref):
    o_ref[...] = jnp.dot(a_ref[...], w_ref[...],
                         preferred_element_type=F32).astype(o_ref.dtype)


def _matmul(a, w, out_dtype, tm, tn):
    m, k = a.shape
    n = w.shape[1]
    return pl.pallas_call(
        _mm_kernel,
        grid=(n // tn, m // tm),
        in_specs=[pl.BlockSpec((tm, k), lambda j, i: (i, 0)),
                  pl.BlockSpec((k, tn), lambda j, i: (0, j))],
        out_specs=pl.BlockSpec((tm, tn), lambda j, i: (i, j)),
        out_shape=jax.ShapeDtypeStruct((m, n), out_dtype),
        compiler_params=_params("parallel", "parallel"),
        name="matmul",
    )(a, w)


def _mm_res_ln_kernel(*refs, n_a):
    a_refs = refs[:n_a]
    w_ref, res_ref, g_ref, b_ref, o_ref, ob_ref = refs[n_a:]
    acc = None
    row = 0
    for a_ref in a_refs:
        k = a_ref.shape[1]
        part = jnp.dot(a_ref[...], w_ref[row:row + k, :], preferred_element_type=F32)
        acc = part if acc is None else acc + part
        row += k
    y = _layer_norm_rows(ALPHA * res_ref[...] + acc, g_ref[...], b_ref[...])
    o_ref[...] = y
    ob_ref[...] = y.astype(BF16)


def _matmul_res_ln(a_list, w, res, g, b, tm):
    m = res.shape[0]
    k, n = w.shape
    in_specs = [pl.BlockSpec((tm, a.shape[1]), lambda i: (i, 0)) for a in a_list]
    in_specs += [pl.BlockSpec((k, n), lambda i: (0, 0)),
                 pl.BlockSpec((tm, n), lambda i: (i, 0)),
                 pl.BlockSpec((1, n), lambda i: (0, 0)),
                 pl.BlockSpec((1, n), lambda i: (0, 0))]
    return pl.pallas_call(
        functools.partial(_mm_res_ln_kernel, n_a=len(a_list)),
        grid=(m // tm,),
        in_specs=in_specs,
        out_specs=[pl.BlockSpec((tm, n), lambda i: (i, 0)),
                   pl.BlockSpec((tm, n), lambda i: (i, 0))],
        out_shape=[jax.ShapeDtypeStruct((m, n), F32),
                   jax.ShapeDtypeStruct((m, n), BF16)],
        compiler_params=_params("parallel"),
        name="matmul_res_ln",
    )(*a_list, w, res, g.reshape(1, n), b.reshape(1, n))


def _mixer_ab_kernel(bg_ref, cg_ref, h_ref, u_ref, v_ref, cgp_ref, hp_ref,
                     cw_ref, lg_ref, lb_ref, ws_ref, bst_ref, o_ref, *, tm, tiles_per_seq):
    i = pl.program_id(0)
    z = cg_ref[...] * h_ref[...]
    zp = cgp_ref[...] * hp_ref[...]
    zp = jnp.where(i % tiles_per_seq == 0, 0.0, zp)
    rows = lax.broadcasted_iota(jnp.int32, (tm, 1), 0)
    last = zp[V7X_SUBLANES - 1:V7X_SUBLANES, :]
    last2 = zp[V7X_SUBLANES - 2:V7X_SUBLANES - 1, :]
    z1 = jnp.where(rows == 0, last, pltpu.roll(z, 1, axis=0))
    z2 = jnp.where(rows == 0, last2, jnp.where(rows == 1, last, pltpu.roll(z, 2, axis=0)))
    cw = cw_ref[...]
    conv = cw[0:1, :] * z2 + cw[1:2, :] * z1 + cw[2:3, :] * z
    o_ref[:, 0:CONV_WIDTH] = (bg_ref[...] * conv).astype(o_ref.dtype)

    u = jax.nn.gelu(u_ref[...])
    vn = _layer_norm_rows(jax.nn.gelu(v_ref[...]), lg_ref[...], lb_ref[...]).astype(BF16)
    tr = lax.broadcasted_iota(jnp.int32, (CHUNK, CHUNK), 0)
    tc = lax.broadcasted_iota(jnp.int32, (CHUNK, CHUNK), 1)
    causal = tr >= tc
    bst = bst_ref[...]
    for g in range(GMLP_GROUPS):
        wg = jnp.where(causal, ws_ref[g], 0.0).astype(BF16)
        bias = bst[:, g:g + 1]
        c0 = g * GMLP_GROUP_DIM
        for c in range(tm // CHUNK):
            r0 = c * CHUNK
            sv = jnp.dot(wg, vn[r0:r0 + CHUNK, c0:c0 + GMLP_GROUP_DIM],
                         preferred_element_type=F32) + bias
            o_ref[r0:r0 + CHUNK, CONV_WIDTH + c0:CONV_WIDTH + c0 + GMLP_GROUP_DIM] = (
                u[r0:r0 + CHUNK, c0:c0 + GMLP_GROUP_DIM] * sv).astype(o_ref.dtype)


def _mixer_ab(proj_ab, conv_w, ln_g, ln_b, ws, bs, seq, tm):
    m = proj_ab.shape[0]
    w = CONV_WIDTH
    halo_blocks = tm // V7X_SUBLANES

    def col(c):
        return pl.BlockSpec((tm, w), lambda i, c=c: (i, c))

    def halo(c):
        return pl.BlockSpec((V7X_SUBLANES, w),
                            lambda i, c=c: (jnp.maximum(i * halo_blocks - 1, 0), c))

    def whole(shape):
        return pl.BlockSpec(shape, lambda i: (0,) * len(shape))

    return pl.pallas_call(
        functools.partial(_mixer_ab_kernel, tm=tm, tiles_per_seq=seq // tm),
        grid=(m // tm,),
        in_specs=[col(0), col(1), col(2), col(3), col(4), halo(1), halo(2),
                  whole((CONV_K, w)), whole((1, w)), whole((1, w)),
                  whole((GMLP_GROUPS, CHUNK, CHUNK)), whole((CHUNK, GMLP_GROUPS))],
        out_specs=pl.BlockSpec((tm, 2 * w), lambda i: (i, 0)),
        out_shape=jax.ShapeDtypeStruct((m, 2 * w), BF16),
        compiler_params=_params("parallel"),
        name="mixer_ab",
    )(proj_ab, proj_ab, proj_ab, proj_ab, proj_ab, proj_ab, proj_ab,
      conv_w, ln_g.reshape(1, w), ln_b.reshape(1, w), ws, jnp.transpose(bs))


def _causal_bucket(n):
    max_exact = NUM_BUCKETS // 2
    nf = jnp.maximum(n, 1).astype(F32)
    large = max_exact + (jnp.log(nf / max_exact) / math.log(MAX_DISTANCE / max_exact)
                         * (NUM_BUCKETS - max_exact)).astype(jnp.int32)
    large = jnp.minimum(large, NUM_BUCKETS - 1)
    return jnp.where(n < max_exact, n, large)


def _near_bias_tiles(rel_bias, tq, tk):
    r = tq // tk
    qpos = jnp.arange(tq)[:, None]
    kpos = jnp.arange(tk)[None, :]
    dist = jnp.stack([qpos - kpos - e * tk for e in range(-1, r)], axis=0)
    bucket = _causal_bucket(jnp.maximum(dist, 0))
    rb = rel_bias.astype(F32)
    rb = jnp.transpose((rb - rb[NUM_BUCKETS - 1]) * LOG2E).reshape(DIFF_HEADS, 2, NUM_BUCKETS)
    tiles = jnp.zeros((DIFF_HEADS, 2) + dist.shape, F32)
    for b in range(NUM_BUCKETS - 1):
        tiles = jnp.where(bucket == b, rb[:, :, b][:, :, None, None, None], tiles)
    return jnp.where(dist >= 0, tiles, MASK_VALUE)


def _diff_attn_kernel(q_ref, k_ref, v_ref, bias_ref, lq_ref, lk_ref, g_ref, o_ref,
                      acc_ref, m_ref, l_ref, t_ref, *, lam_init, tq, tk):
    qi = pl.program_id(2)
    r = tq // tk
    dh = DIFF_HEAD_DIM
    lanes = V7X_LANES
    logit_scale = dh ** -0.5 * LOG2E
    acc_ref[...] = jnp.zeros_like(acc_ref)
    m_ref[...] = jnp.full_like(m_ref, MASK_VALUE)
    l_ref[...] = jnp.zeros_like(l_ref)

    def scores(j, slot):
        start = pl.multiple_of(j * tk, tk)
        for mp in range(2):
            q = q_ref[:, mp * dh:(mp + 1) * dh]
            k = k_ref[pl.ds(start, tk), mp * dh:(mp + 1) * dh]
            t_ref[slot, mp] = lax.dot_general(q, k, (((1,), (1,)), ((), ())),
                                              preferred_element_type=F32) * logit_scale

    def accumulate(j, slot, bias_tile):
        start = pl.multiple_of(j * tk, tk)
        v = v_ref[pl.ds(start, tk), :]
        for mp in range(2):
            t = t_ref[slot, mp]
            if bias_tile is not None:
                t = t + bias_ref[mp, bias_tile]
            m_prev = m_ref[mp]
            m_new = jnp.maximum(m_prev, jnp.max(t, axis=-1, keepdims=True))
            p = jnp.exp2(t - jnp.tile(m_new, (1, tk // lanes)))
            a = jnp.exp2(m_prev - m_new)
            psum = p[:, 0:lanes]
            for c in range(1, tk // lanes):
                psum = psum + p[:, c * lanes:(c + 1) * lanes]
            l_ref[mp] = a * l_ref[mp] + psum
            m_ref[mp] = m_new
            pv = jnp.dot(p.astype(BF16), v, preferred_element_type=F32)
            acc_ref[mp] = jnp.tile(a, (1, 2 * dh // lanes)) * acc_ref[mp] + pv

    n_far = jnp.maximum(qi * r - 1, 0)
    odd = n_far % 2

    @pl.when(odd == 1)
    def _():
        scores(0, 1)
        accumulate(0, 1, None)

    scores(odd, 0)

    @pl.loop(0, n_far // 2)
    def _(i):
        j = odd + 2 * i
        scores(j + 1, 1)
        accumulate(j, 0, None)
        scores(j + 2, 0)
        accumulate(j + 1, 1, None)

    near = [(qi * r + e, e + 1) for e in range(r)]

    def run_near(blocks):
        for n, (j, tile) in enumerate(blocks):
            if n + 1 < len(blocks):
                scores(blocks[n + 1][0], (n + 1) % 2)
            accumulate(j, n % 2, tile)

    @pl.when(qi > 0)
    def _():
        run_near([(qi * r - 1, 0)] + near)

    @pl.when(qi == 0)
    def _():
        run_near(near)

    outs = []
    for mp in range(2):
        l = jnp.sum(l_ref[mp], axis=-1, keepdims=True)
        outs.append(acc_ref[mp] / l)
    sums = jnp.sum(lq_ref[...] * lk_ref[...], axis=-1, keepdims=True)
    ex = jnp.exp(sums)
    lam = ex[0:1, :] - ex[1:2, :] + lam_init
    o = outs[0] - lam * outs[1]
    o = o * lax.rsqrt(jnp.mean(o * o, axis=-1, keepdims=True) + LN_EPS) * g_ref[...]
    o_ref[...] = (o * (1.0 - lam_init)).astype(o_ref.dtype)


def _diff_attention(qkv, bias_tiles, lam_q, lam_k, subln_g, lam_init, bsz, seq, tq, tk):
    m = qkv.shape[0]
    hd = 2 * DIFF_HEAD_DIM
    nq = seq // tq
    r = tq // tk
    return pl.pallas_call(
        functools.partial(_diff_attn_kernel, lam_init=lam_init, tq=tq, tk=tk),
        grid=(bsz, DIFF_HEADS, nq),
        in_specs=[
            pl.BlockSpec((tq, hd), lambda b, h, i: (b * nq + i, h)),
            pl.BlockSpec((seq, hd), lambda b, h, i: (b, DIFF_HEADS + h)),
            pl.BlockSpec((seq, hd), lambda b, h, i: (b, 2 * DIFF_HEADS + h)),
            pl.BlockSpec((None, 2, r + 1, tq, tk), lambda b, h, i: (h, 0, 0, 0, 0)),
            pl.BlockSpec((2, DIFF_HEAD_DIM), lambda b, h, i: (0, 0)),
            pl.BlockSpec((2, DIFF_HEAD_DIM), lambda b, h, i: (0, 0)),
            pl.BlockSpec((1, hd), lambda b, h, i: (0, 0)),
        ],
        out_specs=pl.BlockSpec((tq, hd), lambda b, h, i: (b * nq + i, h)),
        out_shape=jax.ShapeDtypeStruct((m, DIFF_WIDTH), BF16),
        scratch_shapes=[pltpu.VMEM((2, tq, hd), F32),
                        pltpu.VMEM((2, tq, V7X_LANES), F32),
                        pltpu.VMEM((2, tq, V7X_LANES), F32),
                        pltpu.VMEM((2, 2, tq, tk), F32)],
        compiler_params=_params("parallel", "parallel", "arbitrary"),
        name="diff_attention",
    )(qkv, qkv, qkv, bias_tiles, lam_q, lam_k, subln_g.reshape(1, hd))


def _mem_attn_kernel(q_ref, kv_ref, o_ref):
    d = MEM_HEAD_DIM
    scale = d ** -0.5
    for h in range(MEM_HEADS):
        q = q_ref[:, h * d:(h + 1) * d]
        k = kv_ref[:, h * d:(h + 1) * d]
        v = kv_ref[:, D_MODEL + h * d:D_MODEL + (h + 1) * d]
        s = lax.dot_general(q, k, (((1,), (1,)), ((), ())), preferred_element_type=F32) * scale
        s = s - jnp.max(s, axis=-1, keepdims=True)
        p = jnp.exp(s)
        p = p / jnp.sum(p, axis=-1, keepdims=True)
        o_ref[:, h * d:(h + 1) * d] = jnp.dot(p.astype(BF16), v,
                                              preferred_element_type=F32).astype(o_ref.dtype)


def _mem_attention(q, kv, seq, tm):
    m = q.shape[0]
    tiles_per_seq = seq // tm
    return pl.pallas_call(
        _mem_attn_kernel,
        grid=(m // tm,),
        in_specs=[pl.BlockSpec((tm, D_MODEL), lambda i: (i, 0)),
                  pl.BlockSpec((N_MEM, 2 * D_MODEL), lambda i: (i // tiles_per_seq, 0))],
        out_specs=pl.BlockSpec((tm, D_MODEL), lambda i: (i, 0)),
        out_shape=jax.ShapeDtypeStruct((m, D_MODEL), BF16),
        compiler_params=_params("parallel"),
        name="mem_attention",
    )(q, kv)


def _ffn_kernel(xb_ref, wg_ref, wu_ref, wd_ref, res_ref, g_ref, b_ref, o_ref, ob_ref, acc_ref):
    f = pl.program_id(1)
    xb = xb_ref[...]
    gate = jnp.dot(xb, wg_ref[...], preferred_element_type=F32)
    up = jnp.dot(xb, wu_ref[...], preferred_element_type=F32)
    hid = (gate * (1.0 / (1.0 + jnp.exp(-gate))) * up).astype(BF16)
    part = jnp.dot(hid, wd_ref[...], preferred_element_type=F32)

    @pl.when(f == 0)
    def _():
        acc_ref[...] = part

    @pl.when(f > 0)
    def _():
        acc_ref[...] += part

    @pl.when(f == pl.num_programs(1) - 1)
    def _():
        y = _layer_norm_rows(ALPHA * res_ref[...] + acc_ref[...], g_ref[...], b_ref[...])
        o_ref[...] = y
        ob_ref[...] = y.astype(BF16)


def _ffn(xb, w_gu, w_down, res, g, b, tm, tf):
    m, d = xb.shape
    nf = D_FF // tf
    return pl.pallas_call(
        _ffn_kernel,
        grid=(m // tm, nf),
        in_specs=[pl.BlockSpec((tm, d), lambda i, f: (i, 0)),
                  pl.BlockSpec((d, tf), lambda i, f: (0, f)),
                  pl.BlockSpec((d, tf), lambda i, f: (0, nf + f)),
                  pl.BlockSpec((tf, d), lambda i, f: (f, 0)),
                  pl.BlockSpec((tm, d), lambda i, f: (i, 0)),
                  pl.BlockSpec((1, d), lambda i, f: (0, 0)),
                  pl.BlockSpec((1, d), lambda i, f: (0, 0))],
        out_specs=[pl.BlockSpec((tm, d), lambda i, f: (i, 0)),
                   pl.BlockSpec((tm, d), lambda i, f: (i, 0))],
        out_shape=[jax.ShapeDtypeStruct((m, d), F32),
                   jax.ShapeDtypeStruct((m, d), BF16)],
        scratch_shapes=[pltpu.VMEM((tm, d), F32)],
        compiler_params=_params("parallel", "arbitrary"),
        name="ffn",
    )(xb, w_gu, w_gu, w_down, res, g.reshape(1, d), b.reshape(1, d))


ATTN_TQ = 512
ATTN_TK = 512


def kernel(x, mem, rel_bias, w_in, conv_w, gmlp_ln_g, gmlp_ln_b, gmlp_ws, gmlp_bs,
           diff_lambda_q, diff_lambda_k, diff_subln_g, w_mix_out, ln_mix_g, ln_mix_b,
           w_mem_q, w_mem_kv, w_mem_out, ln_mem_g, ln_mem_b,
           w_ffn_gu, w_ffn_down, ln_ffn_g, ln_ffn_b):
    bsz, seq, d = x.shape
    m = bsz * seq
    x = x.reshape(m, d)
    xb = x.astype(BF16)
    memb = mem.reshape(bsz * N_MEM, d).astype(BF16)
    bias_tiles = _near_bias_tiles(rel_bias, ATTN_TQ, ATTN_TK)
    for l in range(DEPTH):
        lam_init = 0.8 - 0.6 * math.exp(-0.3 * l)
        w_in_b = w_in[l].astype(BF16)
        proj_ab = _matmul(xb, w_in_b[:, :AB_COLS], F32, tm=1024, tn=512)
        qkv = _matmul(xb, w_in_b[:, AB_COLS:], BF16, tm=1024, tn=1024)
        y_ab = _mixer_ab(proj_ab, conv_w[l], gmlp_ln_g[l], gmlp_ln_b[l], gmlp_ws[l],
                         gmlp_bs[l], seq, tm=256)
        y_c = _diff_attention(qkv, bias_tiles, diff_lambda_q[l], diff_lambda_k[l],
                              diff_subln_g[l], lam_init, bsz, seq, ATTN_TQ, ATTN_TK)
        x, xb = _matmul_res_ln([y_ab, y_c], w_mix_out[l].astype(BF16), x,
                               ln_mix_g[l], ln_mix_b[l], tm=256)

        q_mem = _matmul(xb, w_mem_q[l].astype(BF16), BF16, tm=1024, tn=1024)
        kv_mem = _matmul(memb, w_mem_kv[l].astype(BF16), BF16, tm=bsz * N_MEM, tn=1024)
        o_mem = _mem_attention(q_mem, kv_mem, seq, tm=512)
        x, xb = _matmul_res_ln([o_mem], w_mem_out[l].astype(BF16), x,
                               ln_mem_g[l], ln_mem_b[l], tm=256)

        x, xb = _ffn(xb, w_ffn_gu[l].astype(BF16), w_ffn_down[l].astype(BF16), x,
                     ln_ffn_g[l], ln_ffn_b[l], tm=512, tf=512)
    return x.reshape(bsz, seq, d)
```

```python
import functools
import math

import jax
import jax.numpy as jnp
from jax import lax
from jax.experimental import pallas as pl
from jax.experimental.pallas import tpu as pltpu

D_MODEL = 2048
DEPTH = 2
N_MEM = 256
MEM_HEADS = 4
MEM_HEAD_DIM = D_MODEL // MEM_HEADS
CONV_WIDTH = D_MODEL // 4
CONV_K = 3
GMLP_WIDTH = D_MODEL // 4
GMLP_GROUPS = 4
GMLP_GROUP_DIM = GMLP_WIDTH // GMLP_GROUPS
CHUNK = 128
DIFF_HEAD_DIM = 128
DIFF_WIDTH = D_MODEL // 2
DIFF_HEADS = DIFF_WIDTH // (2 * DIFF_HEAD_DIM)
AB_COLS = 3 * CONV_WIDTH + 2 * GMLP_WIDTH
QKV_COLS = 3 * DIFF_WIDTH
NUM_BUCKETS = 32
MAX_DISTANCE = 128
D_FF = -(-8 * D_MODEL // (3 * 256)) * 256
ALPHA = (2 * DEPTH) ** 0.25
LN_EPS = 1e-5

V7X_VMEM_BYTES = 64 * 1024 * 1024
V7X_LANES = 128
V7X_SUBLANES = 8
VMEM_LIMIT_BYTES = V7X_VMEM_BYTES - 8 * 1024 * 1024

MASK_VALUE = -1e30
LOG2E = math.log2(math.e)

F32 = jnp.float32
BF16 = jnp.bfloat16


def _params(*semantics):
    return pltpu.CompilerParams(dimension_semantics=semantics,
                                vmem_limit_bytes=VMEM_LIMIT_BYTES)


def _layer_norm_rows(x, g, b):
    mu = jnp.mean(x, axis=-1, keepdims=True)
    xc = x - mu
    var = jnp.mean(xc * xc, axis=-1, keepdims=True)
    return xc * lax.rsqrt(var + LN_EPS) * g + b


def _mm_kernel(a_ref, w_ref, o_ref):
    o_ref[...] = jnp.dot(a_ref[...], w_ref[...],
                         preferred_element_type=F32).astype(o_ref.dtype)


def _matmul(a, w, layer, col0, n, out_dtype, tm, tn):
    m, k = a.shape
    j0 = col0 // tn
    return pl.pallas_call(
        _mm_kernel,
        grid=(m // tm, n // tn),
        in_specs=[pl.BlockSpec((tm, k), lambda i, j: (i, 0)),
                  pl.BlockSpec((None, k, tn), lambda i, j: (layer, 0, j0 + j))],
        out_specs=pl.BlockSpec((tm, tn), lambda i, j: (i, j)),
        out_shape=jax.ShapeDtypeStruct((m, n), out_dtype),
        compiler_params=_params("parallel", "parallel"),
        name="matmul",
    )(a, w)


def _mm_res_ln_kernel(*refs, n_a):
    a_refs = refs[:n_a]
    w_ref, res_ref, g_ref, b_ref, o_ref, ob_ref = refs[n_a:]
    acc = None
    row = 0
    for a_ref in a_refs:
        k = a_ref.shape[1]
        part = jnp.dot(a_ref[...], w_ref[row:row + k, :], preferred_element_type=F32)
        acc = part if acc is None else acc + part
        row += k
    y = _layer_norm_rows(ALPHA * res_ref[...] + acc, g_ref[...], b_ref[...])
    o_ref[...] = y
    ob_ref[...] = y.astype(BF16)


def _matmul_res_ln(a_list, w, layer, res, g, b, tm):
    m = res.shape[0]
    _, k, n = w.shape
    in_specs = [pl.BlockSpec((tm, a.shape[1]), lambda i: (i, 0)) for a in a_list]
    in_specs += [pl.BlockSpec((None, k, n), lambda i: (layer, 0, 0)),
                 pl.BlockSpec((tm, n), lambda i: (i, 0)),
                 pl.BlockSpec((1, n), lambda i: (0, 0)),
                 pl.BlockSpec((1, n), lambda i: (0, 0))]
    return pl.pallas_call(
        functools.partial(_mm_res_ln_kernel, n_a=len(a_list)),
        grid=(m // tm,),
        in_specs=in_specs,
        out_specs=[pl.BlockSpec((tm, n), lambda i: (i, 0)),
                   pl.BlockSpec((tm, n), lambda i: (i, 0))],
        out_shape=[jax.ShapeDtypeStruct((m, n), F32),
                   jax.ShapeDtypeStruct((m, n), BF16)],
        compiler_params=_params("parallel"),
        name="matmul_res_ln",
    )(*a_list, w, res, g.reshape(1, n), b.reshape(1, n))


def _mixer_ab_kernel(bg_ref, cg_ref, h_ref, u_ref, v_ref, cgp_ref, hp_ref,
                     cw_ref, lg_ref, lb_ref, ws_ref, bst_ref, o_ref, *, tm, tiles_per_seq):
    i = pl.program_id(0)
    z = cg_ref[...] * h_ref[...]
    zp = cgp_ref[...] * hp_ref[...]
    zp = jnp.where(i % tiles_per_seq == 0, 0.0, zp)
    rows = lax.broadcasted_iota(jnp.int32, (tm, 1), 0)
    last = zp[V7X_SUBLANES - 1:V7X_SUBLANES, :]
    last2 = zp[V7X_SUBLANES - 2:V7X_SUBLANES - 1, :]
    z1 = jnp.where(rows == 0, last, pltpu.roll(z, 1, axis=0))
    z2 = jnp.where(rows == 0, last2, jnp.where(rows == 1, last, pltpu.roll(z, 2, axis=0)))
    cw = cw_ref[...]
    conv = cw[0:1, :] * z2 + cw[1:2, :] * z1 + cw[2:3, :] * z
    o_ref[:, 0:CONV_WIDTH] = (bg_ref[...] * conv).astype(o_ref.dtype)

    u = jax.nn.gelu(u_ref[...])
    vn = _layer_norm_rows(jax.nn.gelu(v_ref[...]), lg_ref[...], lb_ref[...]).astype(BF16)
    tr = lax.broadcasted_iota(jnp.int32, (CHUNK, CHUNK), 0)
    tc = lax.broadcasted_iota(jnp.int32, (CHUNK, CHUNK), 1)
    causal = tr >= tc
    bst = bst_ref[...]
    for g in range(GMLP_GROUPS):
        wg = jnp.where(causal, ws_ref[g], 0.0).astype(BF16)
        bias = bst[:, g:g + 1]
        c0 = g * GMLP_GROUP_DIM
        for c in range(tm // CHUNK):
            r0 = c * CHUNK
            sv = jnp.dot(wg, vn[r0:r0 + CHUNK, c0:c0 + GMLP_GROUP_DIM],
                         preferred_element_type=F32) + bias
            o_ref[r0:r0 + CHUNK, CONV_WIDTH + c0:CONV_WIDTH + c0 + GMLP_GROUP_DIM] = (
                u[r0:r0 + CHUNK, c0:c0 + GMLP_GROUP_DIM] * sv).astype(o_ref.dtype)


def _mixer_ab(proj_ab, conv_w, ln_g, ln_b, ws, bs, seq, tm):
    m = proj_ab.shape[0]
    w = CONV_WIDTH
    halo_blocks = tm // V7X_SUBLANES

    def col(c):
        return pl.BlockSpec((tm, w), lambda i, c=c: (i, c))

    def halo(c):
        return pl.BlockSpec((V7X_SUBLANES, w),
                            lambda i, c=c: (jnp.maximum(i * halo_blocks - 1, 0), c))

    def whole(shape):
        return pl.BlockSpec(shape, lambda i: (0,) * len(shape))

    return pl.pallas_call(
        functools.partial(_mixer_ab_kernel, tm=tm, tiles_per_seq=seq // tm),
        grid=(m // tm,),
        in_specs=[col(0), col(1), col(2), col(3), col(4), halo(1), halo(2),
                  whole((CONV_K, w)), whole((1, w)), whole((1, w)),
                  whole((GMLP_GROUPS, CHUNK, CHUNK)), whole((CHUNK, GMLP_GROUPS))],
        out_specs=pl.BlockSpec((tm, 2 * w), lambda i: (i, 0)),
        out_shape=jax.ShapeDtypeStruct((m, 2 * w), BF16),
        compiler_params=_params("parallel"),
        name="mixer_ab",
    )(proj_ab, proj_ab, proj_ab, proj_ab, proj_ab, proj_ab, proj_ab,
      conv_w, ln_g.reshape(1, w), ln_b.reshape(1, w), ws, jnp.transpose(bs))


def _causal_bucket(n):
    max_exact = NUM_BUCKETS // 2
    nf = jnp.maximum(n, 1).astype(F32)
    large = max_exact + (jnp.log(nf / max_exact) / math.log(MAX_DISTANCE / max_exact)
                         * (NUM_BUCKETS - max_exact)).astype(jnp.int32)
    large = jnp.minimum(large, NUM_BUCKETS - 1)
    return jnp.where(n < max_exact, n, large)


def _near_bias_tiles(rel_bias, tq, tk):
    r = tq // tk
    qpos = jnp.arange(tq)[:, None]
    kpos = jnp.arange(tk)[None, :]
    dist = jnp.stack([qpos - kpos - e * tk for e in range(-1, r)], axis=0)
    bucket = _causal_bucket(jnp.maximum(dist, 0))
    rb = rel_bias.astype(F32)
    rb = jnp.transpose((rb - rb[NUM_BUCKETS - 1]) * LOG2E).reshape(DIFF_HEADS, 2, NUM_BUCKETS)
    tiles = jnp.zeros((DIFF_HEADS, 2) + dist.shape, F32)
    for b in range(NUM_BUCKETS - 1):
        tiles = jnp.where(bucket == b, rb[:, :, b][:, :, None, None, None], tiles)
    return jnp.where(dist >= 0, tiles, MASK_VALUE)


def _diff_attn_kernel(q_ref, k_ref, v_ref, bias_ref, lq_ref, lk_ref, g_ref, o_ref,
                      acc_ref, m_ref, l_ref, t_ref, pm_ref, a_ref, p_ref, *, lam_init, tq, tk):
    qi = pl.program_id(2)
    r = tq // tk
    dh = DIFF_HEAD_DIM
    lanes = V7X_LANES
    logit_scale = dh ** -0.5 * LOG2E
    acc_ref[...] = jnp.zeros_like(acc_ref)
    m_ref[...] = jnp.full_like(m_ref, MASK_VALUE)
    l_ref[...] = jnp.zeros_like(l_ref)

    def scores(j, slot):
        start = pl.multiple_of(j * tk, tk)
        for mp in range(2):
            q = q_ref[:, mp * dh:(mp + 1) * dh]
            k = k_ref[pl.ds(start, tk), mp * dh:(mp + 1) * dh]
            t = lax.dot_general(q, k, (((1,), (1,)), ((), ())),
                                preferred_element_type=F32) * logit_scale
            t_ref[slot, mp] = t
            pmax = t[:, 0:lanes]
            for c in range(1, tk // lanes):
                pmax = jnp.maximum(pmax, t[:, c * lanes:(c + 1) * lanes])
            pm_ref[slot, mp] = pmax

    def accumulate(j, slot, bias_tile):
        start = pl.multiple_of(j * tk, tk)
        v = v_ref[pl.ds(start, tk), :]
        for mp in range(2):
            for s0 in range(0, tq, SOFTMAX_STRIP_ROWS):
                rows = slice(s0, s0 + SOFTMAX_STRIP_ROWS)
                t = t_ref[slot, mp, rows, :]
                if bias_tile is None:
                    t_max = jnp.max(pm_ref[slot, mp, rows, :], axis=-1, keepdims=True)
                else:
                    t = t + bias_ref[mp, bias_tile, rows, :]
                    t_max = jnp.max(t, axis=-1, keepdims=True)
                m_prev = m_ref[mp, rows, :]
                m_new = jnp.maximum(m_prev, t_max)
                p = jnp.exp2(t - jnp.tile(m_new, (1, tk // lanes)))
                a = jnp.exp2(m_prev - m_new)
                psum = p[:, 0:lanes]
                for c in range(1, tk // lanes):
                    psum = psum + p[:, c * lanes:(c + 1) * lanes]
                l_ref[mp, rows, :] = a * l_ref[mp, rows, :] + psum
                m_ref[mp, rows, :] = m_new
                a_ref[mp, rows, :] = a
                p_ref[slot, mp, rows, :] = p.astype(BF16)
            pv = jnp.dot(p_ref[slot, mp], v, preferred_element_type=F32)
            acc_ref[mp] = jnp.tile(a_ref[mp], (1, 2 * dh // lanes)) * acc_ref[mp] + pv

    n_far = jnp.maximum(qi * r - 1, 0)
    odd = n_far % 2

    @pl.when(odd == 1)
    def _():
        scores(0, 1)
        accumulate(0, 1, None)

    scores(odd, 0)

    @pl.loop(0, n_far // 2)
    def _(i):
        j = odd + 2 * i
        scores(j + 1, 1)
        accumulate(j, 0, None)
        scores(j + 2, 0)
        accumulate(j + 1, 1, None)

    near = [(qi * r + e, e + 1) for e in range(r)]

    def run_near(blocks):
        for n, (j, tile) in enumerate(blocks):
            if n + 1 < len(blocks):
                scores(blocks[n + 1][0], (n + 1) % 2)
            accumulate(j, n % 2, tile)

    @pl.when(qi > 0)
    def _():
        run_near([(qi * r - 1, 0)] + near)

    @pl.when(qi == 0)
    def _():
        run_near(near)

    outs = []
    for mp in range(2):
        l = jnp.sum(l_ref[mp], axis=-1, keepdims=True)
        outs.append(acc_ref[mp] / l)
    sums = jnp.sum(lq_ref[...] * lk_ref[...], axis=-1, keepdims=True)
    ex = jnp.exp(sums)
    lam = ex[0:1, :] - ex[1:2, :] + lam_init
    o = outs[0] - lam * outs[1]
    o = o * lax.rsqrt(jnp.mean(o * o, axis=-1, keepdims=True) + LN_EPS) * g_ref[...]
    o_ref[...] = (o * (1.0 - lam_init)).astype(o_ref.dtype)


def _diff_attention(qkv, bias_tiles, lam_q, lam_k, subln_g, lam_init, bsz, seq, tq, tk):
    m = qkv.shape[0]
    hd = 2 * DIFF_HEAD_DIM
    nq = seq // tq
    r = tq // tk
    return pl.pallas_call(
        functools.partial(_diff_attn_kernel, lam_init=lam_init, tq=tq, tk=tk),
        grid=(bsz, DIFF_HEADS, nq),
        in_specs=[
            pl.BlockSpec((tq, hd), lambda b, h, i: (b * nq + i, h)),
            pl.BlockSpec((seq, hd), lambda b, h, i: (b, DIFF_HEADS + h)),
            pl.BlockSpec((seq, hd), lambda b, h, i: (b, 2 * DIFF_HEADS + h)),
            pl.BlockSpec((None, 2, r + 1, tq, tk), lambda b, h, i: (h, 0, 0, 0, 0)),
            pl.BlockSpec((2, DIFF_HEAD_DIM), lambda b, h, i: (0, 0)),
            pl.BlockSpec((2, DIFF_HEAD_DIM), lambda b, h, i: (0, 0)),
            pl.BlockSpec((1, hd), lambda b, h, i: (0, 0)),
        ],
        out_specs=pl.BlockSpec((tq, hd), lambda b, h, i: (b * nq + i, h)),
        out_shape=jax.ShapeDtypeStruct((m, DIFF_WIDTH), BF16),
        scratch_shapes=[pltpu.VMEM((2, tq, hd), F32),
                        pltpu.VMEM((2, tq, V7X_LANES), F32),
                        pltpu.VMEM((2, tq, V7X_LANES), F32),
                        pltpu.VMEM((2, 2, tq, tk), F32),
                        pltpu.VMEM((2, 2, tq, V7X_LANES), F32),
                        pltpu.VMEM((2, tq, V7X_LANES), F32),
                        pltpu.VMEM((2, 2, tq, tk), BF16)],
        compiler_params=_params("parallel", "parallel", "arbitrary"),
        name="diff_attention",
    )(qkv, qkv, qkv, bias_tiles, lam_q, lam_k, subln_g.reshape(1, hd))


def _mem_attn_kernel(q_ref, kv_ref, o_ref):
    d = MEM_HEAD_DIM
    scale = d ** -0.5
    for h in range(MEM_HEADS):
        q = q_ref[:, h * d:(h + 1) * d]
        k = kv_ref[:, h * d:(h + 1) * d]
        v = kv_ref[:, D_MODEL + h * d:D_MODEL + (h + 1) * d]
        s = lax.dot_general(q, k, (((1,), (1,)), ((), ())), preferred_element_type=F32) * scale
        s = s - jnp.max(s, axis=-1, keepdims=True)
        p = jnp.exp(s)
        p = p / jnp.sum(p, axis=-1, keepdims=True)
        o_ref[:, h * d:(h + 1) * d] = jnp.dot(p.astype(BF16), v,
                                              preferred_element_type=F32).astype(o_ref.dtype)


def _mem_attention(q, kv, seq, tm):
    m = q.shape[0]
    tiles_per_seq = seq // tm
    return pl.pallas_call(
        _mem_attn_kernel,
        grid=(m // tm,),
        in_specs=[pl.BlockSpec((tm, D_MODEL), lambda i: (i, 0)),
                  pl.BlockSpec((N_MEM, 2 * D_MODEL), lambda i: (i // tiles_per_seq, 0))],
        out_specs=pl.BlockSpec((tm, D_MODEL), lambda i: (i, 0)),
        out_shape=jax.ShapeDtypeStruct((m, D_MODEL), BF16),
        compiler_params=_params("parallel"),
        name="mem_attention",
    )(q, kv)


def _ffn_kernel(xb_ref, wg_ref, wu_ref, wd_ref, res_ref, g_ref, b_ref, o_ref, ob_ref, acc_ref):
    f = pl.program_id(1)

    @pl.when(f == 0)
    def _():
        acc_ref[...] = jnp.zeros_like(acc_ref)

    xb = xb_ref[...]
    gate = jnp.dot(xb, wg_ref[...], preferred_element_type=F32)
    up = jnp.dot(xb, wu_ref[...], preferred_element_type=F32)
    hid = (gate * (1.0 / (1.0 + jnp.exp(-gate))) * up).astype(BF16)
    acc_ref[...] += jnp.dot(hid, wd_ref[...], preferred_element_type=F32)

    @pl.when(f == pl.num_programs(1) - 1)
    def _():
        y = _layer_norm_rows(ALPHA * res_ref[...] + acc_ref[...], g_ref[...], b_ref[...])
        o_ref[...] = y
        ob_ref[...] = y.astype(BF16)


def _ffn(xb, w_gu, w_down, layer, res, g, b, tm, tf):
    m, d = xb.shape
    nf = D_FF // tf
    return pl.pallas_call(
        _ffn_kernel,
        grid=(m // tm, nf),
        in_specs=[pl.BlockSpec((tm, d), lambda i, f: (i, 0)),
                  pl.BlockSpec((None, d, tf), lambda i, f: (layer, 0, f)),
                  pl.BlockSpec((None, d, tf), lambda i, f: (layer, 0, nf + f)),
                  pl.BlockSpec((None, tf, d), lambda i, f: (layer, f, 0)),
                  pl.BlockSpec((tm, d), lambda i, f: (i, 0)),
                  pl.BlockSpec((1, d), lambda i, f: (0, 0)),
                  pl.BlockSpec((1, d), lambda i, f: (0, 0))],
        out_specs=[pl.BlockSpec((tm, d), lambda i, f: (i, 0)),
                   pl.BlockSpec((tm, d), lambda i, f: (i, 0))],
        out_shape=[jax.ShapeDtypeStruct((m, d), F32),
                   jax.ShapeDtypeStruct((m, d), BF16)],
        scratch_shapes=[pltpu.VMEM((tm, d), F32)],
        compiler_params=_params("parallel", "arbitrary"),
        name="ffn",
    )(xb, w_gu, w_gu, w_down, res, g.reshape(1, d), b.reshape(1, d))


ATTN_TQ = 512
ATTN_TK = 512
SOFTMAX_STRIP_ROWS = 64


def kernel(x, mem, rel_bias, w_in, conv_w, gmlp_ln_g, gmlp_ln_b, gmlp_ws, gmlp_bs,
           diff_lambda_q, diff_lambda_k, diff_subln_g, w_mix_out, ln_mix_g, ln_mix_b,
           w_mem_q, w_mem_kv, w_mem_out, ln_mem_g, ln_mem_b,
           w_ffn_gu, w_ffn_down, ln_ffn_g, ln_ffn_b):
    bsz, seq, d = x.shape
    m = bsz * seq
    x = x.reshape(m, d)
    xb = x.astype(BF16)
    memb = mem.reshape(bsz * N_MEM, d).astype(BF16)
    bias_tiles = _near_bias_tiles(rel_bias, ATTN_TQ, ATTN_TK)
    w_in, w_mix_out, w_mem_q, w_mem_kv, w_mem_out, w_ffn_gu, w_ffn_down = (
        w.astype(BF16) for w in (w_in, w_mix_out, w_mem_q, w_mem_kv, w_mem_out,
                                 w_ffn_gu, w_ffn_down))
    for l in range(DEPTH):
        lam_init = 0.8 - 0.6 * math.exp(-0.3 * l)
        proj_ab = _matmul(xb, w_in, l, 0, AB_COLS, F32, tm=2048, tn=512)
        qkv = _matmul(xb, w_in, l, AB_COLS, QKV_COLS, BF16, tm=2048, tn=512)
        y_ab = _mixer_ab(proj_ab, conv_w[l], gmlp_ln_g[l], gmlp_ln_b[l], gmlp_ws[l],
                         gmlp_bs[l], seq, tm=256)
        y_c = _diff_attention(qkv, bias_tiles, diff_lambda_q[l], diff_lambda_k[l],
                              diff_subln_g[l], lam_init, bsz, seq, ATTN_TQ, ATTN_TK)
        x, xb = _matmul_res_ln([y_ab, y_c], w_mix_out, l, x, ln_mix_g[l], ln_mix_b[l], tm=512)

        q_mem = _matmul(xb, w_mem_q, l, 0, D_MODEL, BF16, tm=2048, tn=1024)
        kv_mem = _matmul(memb, w_mem_kv, l, 0, 2 * D_MODEL, BF16, tm=bsz * N_MEM, tn=1024)
        o_mem = _mem_attention(q_mem, kv_mem, seq, tm=512)
        x, xb = _matmul_res_ln([o_mem], w_mem_out, l, x, ln_mem_g[l], ln_mem_b[l], tm=512)

        x, xb = _ffn(xb, w_ffn_gu, w_ffn_down, l, x, ln_ffn_g[l], ln_ffn_b[l], tm=512, tf=512)
    return x.reshape(bsz, seq, d)
```

```python
import functools
import math

import jax
import jax.numpy as jnp
from jax import lax
from jax.experimental import pallas as pl
from jax.experimental.pallas import tpu as pltpu

D_MODEL = 2048
DEPTH = 2
N_MEM = 256
MEM_HEADS = 4
MEM_HEAD_DIM = D_MODEL // MEM_HEADS
CONV_WIDTH = D_MODEL // 4
CONV_K = 3
GMLP_WIDTH = D_MODEL // 4
GMLP_GROUPS = 4
GMLP_GROUP_DIM = GMLP_WIDTH // GMLP_GROUPS
CHUNK = 128
DIFF_HEAD_DIM = 128
DIFF_WIDTH = D_MODEL // 2
DIFF_HEADS = DIFF_WIDTH // (2 * DIFF_HEAD_DIM)
AB_COLS = 3 * CONV_WIDTH + 2 * GMLP_WIDTH
QKV_COLS = 3 * DIFF_WIDTH
NUM_BUCKETS = 32
MAX_DISTANCE = 128
D_FF = -(-8 * D_MODEL // (3 * 256)) * 256
ALPHA = (2 * DEPTH) ** 0.25
LN_EPS = 1e-5

V7X_VMEM_BYTES = 64 * 1024 * 1024
V7X_LANES = 128
V7X_SUBLANES = 8
VMEM_LIMIT_BYTES = V7X_VMEM_BYTES - 8 * 1024 * 1024

MASK_VALUE = -1e30
LOG2E = math.log2(math.e)

F32 = jnp.float32
BF16 = jnp.bfloat16


def _params(*semantics):
    return pltpu.CompilerParams(dimension_semantics=semantics,
                                vmem_limit_bytes=VMEM_LIMIT_BYTES)


def _layer_norm_rows(x, g, b):
    mu = jnp.mean(x, axis=-1, keepdims=True)
    xc = x - mu
    var = jnp.mean(xc * xc, axis=-1, keepdims=True)
    return xc * lax.rsqrt(var + LN_EPS) * g + b


def _mm_kernel(a_ref, w_ref, o_ref):
    o_ref[...] = jnp.dot(a_ref[...].astype(BF16), w_ref[...].astype(BF16),
                         preferred_element_type=F32).astype(o_ref.dtype)


def _matmul(a, w, layer, col0, n, out_dtype, tm, tn):
    m, k = a.shape
    j0 = col0 // tn
    return pl.pallas_call(
        _mm_kernel,
        grid=(m // tm, n // tn),
        in_specs=[pl.BlockSpec((tm, k), lambda i, j: (i, 0)),
                  pl.BlockSpec((None, k, tn), lambda i, j: (layer, 0, j0 + j))],
        out_specs=pl.BlockSpec((tm, tn), lambda i, j: (i, j)),
        out_shape=jax.ShapeDtypeStruct((m, n), out_dtype),
        compiler_params=_params("parallel", "parallel"),
        name="matmul",
    )(a, w)


def _mm_res_ln_kernel(*refs, n_a):
    a_refs = refs[:n_a]
    w_ref, res_ref, g_ref, b_ref, o_ref, ob_ref = refs[n_a:]
    acc = None
    row = 0
    for a_ref in a_refs:
        k = a_ref.shape[1]
        part = jnp.dot(a_ref[...], w_ref[row:row + k, :], preferred_element_type=F32)
        acc = part if acc is None else acc + part
        row += k
    y = _layer_norm_rows(ALPHA * res_ref[...] + acc, g_ref[...], b_ref[...])
    o_ref[...] = y
    ob_ref[...] = y.astype(BF16)


def _matmul_res_ln(a_list, w, layer, res, g, b, tm):
    m = res.shape[0]
    _, k, n = w.shape
    in_specs = [pl.BlockSpec((tm, a.shape[1]), lambda i: (i, 0)) for a in a_list]
    in_specs += [pl.BlockSpec((None, k, n), lambda i: (layer, 0, 0)),
                 pl.BlockSpec((tm, n), lambda i: (i, 0)),
                 pl.BlockSpec((1, n), lambda i: (0, 0)),
                 pl.BlockSpec((1, n), lambda i: (0, 0))]
    return pl.pallas_call(
        functools.partial(_mm_res_ln_kernel, n_a=len(a_list)),
        grid=(m // tm,),
        in_specs=in_specs,
        out_specs=[pl.BlockSpec((tm, n), lambda i: (i, 0)),
                   pl.BlockSpec((tm, n), lambda i: (i, 0))],
        out_shape=[jax.ShapeDtypeStruct((m, n), F32),
                   jax.ShapeDtypeStruct((m, n), BF16)],
        compiler_params=_params("parallel"),
        name="matmul_res_ln",
    )(*a_list, w, res, g.reshape(1, n), b.reshape(1, n))


def _mixer_ab_kernel(bg_ref, cg_ref, h_ref, u_ref, v_ref, cgp_ref, hp_ref,
                     cw_ref, lg_ref, lb_ref, ws_ref, bst_ref, o_ref, *, tm, tiles_per_seq):
    i = pl.program_id(0)
    z = cg_ref[...] * h_ref[...]
    zp = cgp_ref[...] * hp_ref[...]
    zp = jnp.where(i % tiles_per_seq == 0, 0.0, zp)
    rows = lax.broadcasted_iota(jnp.int32, (tm, 1), 0)
    last = zp[V7X_SUBLANES - 1:V7X_SUBLANES, :]
    last2 = zp[V7X_SUBLANES - 2:V7X_SUBLANES - 1, :]
    z1 = jnp.where(rows == 0, last, pltpu.roll(z, 1, axis=0))
    z2 = jnp.where(rows == 0, last2, jnp.where(rows == 1, last, pltpu.roll(z, 2, axis=0)))
    cw = cw_ref[...]
    conv = cw[0:1, :] * z2 + cw[1:2, :] * z1 + cw[2:3, :] * z
    o_ref[:, 0:CONV_WIDTH] = (bg_ref[...] * conv).astype(o_ref.dtype)

    u = jax.nn.gelu(u_ref[...])
    vn = _layer_norm_rows(jax.nn.gelu(v_ref[...]), lg_ref[...], lb_ref[...]).astype(BF16)
    tr = lax.broadcasted_iota(jnp.int32, (CHUNK, CHUNK), 0)
    tc = lax.broadcasted_iota(jnp.int32, (CHUNK, CHUNK), 1)
    causal = tr >= tc
    bst = bst_ref[...]
    for g in range(GMLP_GROUPS):
        wg = jnp.where(causal, ws_ref[g], 0.0).astype(BF16)
        bias = bst[:, g:g + 1]
        c0 = g * GMLP_GROUP_DIM
        for c in range(tm // CHUNK):
            r0 = c * CHUNK
            sv = jnp.dot(wg, vn[r0:r0 + CHUNK, c0:c0 + GMLP_GROUP_DIM],
                         preferred_element_type=F32) + bias
            o_ref[r0:r0 + CHUNK, CONV_WIDTH + c0:CONV_WIDTH + c0 + GMLP_GROUP_DIM] = (
                u[r0:r0 + CHUNK, c0:c0 + GMLP_GROUP_DIM] * sv).astype(o_ref.dtype)


def _mixer_ab(proj_ab, conv_w, ln_g, ln_b, ws, bs, seq, tm):
    m = proj_ab.shape[0]
    w = CONV_WIDTH
    halo_blocks = tm // V7X_SUBLANES

    def col(c):
        return pl.BlockSpec((tm, w), lambda i, c=c: (i, c))

    def halo(c):
        return pl.BlockSpec((V7X_SUBLANES, w),
                            lambda i, c=c: (jnp.maximum(i * halo_blocks - 1, 0), c))

    def whole(shape):
        return pl.BlockSpec(shape, lambda i: (0,) * len(shape))

    return pl.pallas_call(
        functools.partial(_mixer_ab_kernel, tm=tm, tiles_per_seq=seq // tm),
        grid=(m // tm,),
        in_specs=[col(0), col(1), col(2), col(3), col(4), halo(1), halo(2),
                  whole((CONV_K, w)), whole((1, w)), whole((1, w)),
                  whole((GMLP_GROUPS, CHUNK, CHUNK)), whole((CHUNK, GMLP_GROUPS))],
        out_specs=pl.BlockSpec((tm, 2 * w), lambda i: (i, 0)),
        out_shape=jax.ShapeDtypeStruct((m, 2 * w), BF16),
        compiler_params=_params("parallel"),
        name="mixer_ab",
    )(proj_ab, proj_ab, proj_ab, proj_ab, proj_ab, proj_ab, proj_ab,
      conv_w, ln_g.reshape(1, w), ln_b.reshape(1, w), ws, jnp.transpose(bs))


def _causal_bucket(n):
    max_exact = NUM_BUCKETS // 2
    nf = jnp.maximum(n, 1).astype(F32)
    large = max_exact + (jnp.log(nf / max_exact) / math.log(MAX_DISTANCE / max_exact)
                         * (NUM_BUCKETS - max_exact)).astype(jnp.int32)
    large = jnp.minimum(large, NUM_BUCKETS - 1)
    return jnp.where(n < max_exact, n, large)


def _near_bias_tiles(rel_bias, tq, tk):
    r = tq // tk
    qpos = jnp.arange(tq)[:, None]
    kpos = jnp.arange(tk)[None, :]
    dist = jnp.stack([qpos - kpos - e * tk for e in range(-1, r)], axis=0)
    bucket = _causal_bucket(jnp.maximum(dist, 0))
    rb = rel_bias.astype(F32)
    rb = jnp.transpose((rb - rb[NUM_BUCKETS - 1]) * LOG2E).reshape(DIFF_HEADS, 2, NUM_BUCKETS)
    tiles = jnp.zeros((DIFF_HEADS, 2) + dist.shape, F32)
    for b in range(NUM_BUCKETS - 1):
        tiles = jnp.where(bucket == b, rb[:, :, b][:, :, None, None, None], tiles)
    return jnp.where(dist >= 0, tiles, MASK_VALUE)


def _diff_attn_kernel(q_ref, k_ref, v_ref, bias_ref, lq_ref, lk_ref, g_ref, o_ref,
                      acc_ref, m_ref, l_ref, t_ref, pm_ref, *, lam_init, tq, tk):
    qi = pl.program_id(2)
    r = tq // tk
    dh = DIFF_HEAD_DIM
    lanes = V7X_LANES
    logit_scale = dh ** -0.5 * LOG2E
    acc_ref[...] = jnp.zeros_like(acc_ref)
    m_ref[...] = jnp.full_like(m_ref, MASK_VALUE)
    l_ref[...] = jnp.zeros_like(l_ref)

    def scores(j, slot):
        start = pl.multiple_of(j * tk, tk)
        for mp in range(2):
            q = q_ref[:, mp * dh:(mp + 1) * dh]
            k = k_ref[pl.ds(start, tk), mp * dh:(mp + 1) * dh]
            t = lax.dot_general(q, k, (((1,), (1,)), ((), ())),
                                preferred_element_type=F32) * logit_scale
            t_ref[slot, mp] = t
            pmax = t[:, 0:lanes]
            for c in range(1, tk // lanes):
                pmax = jnp.maximum(pmax, t[:, c * lanes:(c + 1) * lanes])
            pm_ref[slot, mp] = pmax

    def accumulate(j, slot, bias_tile):
        start = pl.multiple_of(j * tk, tk)
        v = v_ref[pl.ds(start, tk), :]
        for mp in range(2):
            t = t_ref[slot, mp]
            if bias_tile is None:
                t_max = jnp.max(pm_ref[slot, mp], axis=-1, keepdims=True)
            else:
                t = t + bias_ref[mp, bias_tile]
                t_max = jnp.max(t, axis=-1, keepdims=True)
            m_prev = m_ref[mp]
            m_new = jnp.maximum(m_prev, t_max)
            p = jnp.exp2(t - jnp.tile(m_new, (1, tk // lanes)))
            a = jnp.exp2(m_prev - m_new)
            psum = p[:, 0:lanes]
            for c in range(1, tk // lanes):
                psum = psum + p[:, c * lanes:(c + 1) * lanes]
            l_ref[mp] = a * l_ref[mp] + psum
            m_ref[mp] = m_new
            pv = jnp.dot(p.astype(BF16), v, preferred_element_type=F32)
            acc_ref[mp] = jnp.tile(a, (1, 2 * dh // lanes)) * acc_ref[mp] + pv

    n_far = jnp.maximum(qi * r - 1, 0)
    odd = n_far % 2

    @pl.when(odd == 1)
    def _():
        scores(0, 1)
        accumulate(0, 1, None)

    scores(odd, 0)

    @pl.loop(0, n_far // 2)
    def _(i):
        j = odd + 2 * i
        scores(j + 1, 1)
        accumulate(j, 0, None)
        scores(j + 2, 0)
        accumulate(j + 1, 1, None)

    near = [(qi * r + e, e + 1) for e in range(r)]

    def run_near(blocks):
        for n, (j, tile) in enumerate(blocks):
            if n + 1 < len(blocks):
                scores(blocks[n + 1][0], (n + 1) % 2)
            accumulate(j, n % 2, tile)

    @pl.when(qi > 0)
    def _():
        run_near([(qi * r - 1, 0)] + near)

    @pl.when(qi == 0)
    def _():
        run_near(near)

    outs = []
    for mp in range(2):
        l = jnp.sum(l_ref[mp], axis=-1, keepdims=True)
        outs.append(acc_ref[mp] / l)
    sums = jnp.sum(lq_ref[...] * lk_ref[...], axis=-1, keepdims=True)
    ex = jnp.exp(sums)
    lam = ex[0:1, :] - ex[1:2, :] + lam_init
    o = outs[0] - lam * outs[1]
    o = o * lax.rsqrt(jnp.mean(o * o, axis=-1, keepdims=True) + LN_EPS) * g_ref[...]
    o_ref[...] = (o * (1.0 - lam_init)).astype(o_ref.dtype)


def _diff_attention(qkv, bias_tiles, lam_q, lam_k, subln_g, lam_init, bsz, seq, tq, tk):
    m = qkv.shape[0]
    hd = 2 * DIFF_HEAD_DIM
    nq = seq // tq
    r = tq // tk
    return pl.pallas_call(
        functools.partial(_diff_attn_kernel, lam_init=lam_init, tq=tq, tk=tk),
        grid=(bsz, DIFF_HEADS, nq),
        in_specs=[
            pl.BlockSpec((tq, hd), lambda b, h, i: (b * nq + i, h)),
            pl.BlockSpec((seq, hd), lambda b, h, i: (b, DIFF_HEADS + h)),
            pl.BlockSpec((seq, hd), lambda b, h, i: (b, 2 * DIFF_HEADS + h)),
            pl.BlockSpec((None, 2, r + 1, tq, tk), lambda b, h, i: (h, 0, 0, 0, 0)),
            pl.BlockSpec((2, DIFF_HEAD_DIM), lambda b, h, i: (0, 0)),
            pl.BlockSpec((2, DIFF_HEAD_DIM), lambda b, h, i: (0, 0)),
            pl.BlockSpec((1, hd), lambda b, h, i: (0, 0)),
        ],
        out_specs=pl.BlockSpec((tq, hd), lambda b, h, i: (b * nq + i, h)),
        out_shape=jax.ShapeDtypeStruct((m, DIFF_WIDTH), BF16),
        scratch_shapes=[pltpu.VMEM((2, tq, hd), F32),
                        pltpu.VMEM((2, tq, V7X_LANES), F32),
                        pltpu.VMEM((2, tq, V7X_LANES), F32),
                        pltpu.VMEM((2, 2, tq, tk), F32),
                        pltpu.VMEM((2, 2, tq, V7X_LANES), F32)],
        compiler_params=_params("parallel", "parallel", "arbitrary"),
        name="diff_attention",
    )(qkv, qkv, qkv, bias_tiles, lam_q, lam_k, subln_g.reshape(1, hd))


def _mem_attn_kernel(q_ref, kv_ref, o_ref):
    d = MEM_HEAD_DIM
    scale = d ** -0.5
    for h in range(MEM_HEADS):
        q = q_ref[:, h * d:(h + 1) * d]
        k = kv_ref[:, h * d:(h + 1) * d]
        v = kv_ref[:, D_MODEL + h * d:D_MODEL + (h + 1) * d]
        s = lax.dot_general(q, k, (((1,), (1,)), ((), ())), preferred_element_type=F32) * scale
        s = s - jnp.max(s, axis=-1, keepdims=True)
        p = jnp.exp(s)
        p = p / jnp.sum(p, axis=-1, keepdims=True)
        o_ref[:, h * d:(h + 1) * d] = jnp.dot(p.astype(BF16), v,
                                              preferred_element_type=F32).astype(o_ref.dtype)


def _mem_attention(q, kv, seq, tm):
    m = q.shape[0]
    tiles_per_seq = seq // tm
    return pl.pallas_call(
        _mem_attn_kernel,
        grid=(m // tm,),
        in_specs=[pl.BlockSpec((tm, D_MODEL), lambda i: (i, 0)),
                  pl.BlockSpec((N_MEM, 2 * D_MODEL), lambda i: (i // tiles_per_seq, 0))],
        out_specs=pl.BlockSpec((tm, D_MODEL), lambda i: (i, 0)),
        out_shape=jax.ShapeDtypeStruct((m, D_MODEL), BF16),
        compiler_params=_params("parallel"),
        name="mem_attention",
    )(q, kv)


def _ffn_kernel(xb_ref, wg_ref, wu_ref, wd_ref, res_ref, g_ref, b_ref, o_ref, ob_ref, acc_ref):
    f = pl.program_id(1)

    @pl.when(f == 0)
    def _():
        acc_ref[...] = jnp.zeros_like(acc_ref)

    xb = xb_ref[...]
    gate = jnp.dot(xb, wg_ref[...], preferred_element_type=F32)
    up = jnp.dot(xb, wu_ref[...], preferred_element_type=F32)
    hid = (gate * (1.0 / (1.0 + jnp.exp(-gate))) * up).astype(BF16)
    acc_ref[...] += jnp.dot(hid, wd_ref[...], preferred_element_type=F32)

    @pl.when(f == pl.num_programs(1) - 1)
    def _():
        y = _layer_norm_rows(ALPHA * res_ref[...] + acc_ref[...], g_ref[...], b_ref[...])
        o_ref[...] = y
        ob_ref[...] = y.astype(BF16)


def _ffn(xb, w_gu, w_down, layer, res, g, b, tm, tf):
    m, d = xb.shape
    nf = D_FF // tf
    return pl.pallas_call(
        _ffn_kernel,
        grid=(m // tm, nf),
        in_specs=[pl.BlockSpec((tm, d), lambda i, f: (i, 0)),
                  pl.BlockSpec((None, d, tf), lambda i, f: (layer, 0, f)),
                  pl.BlockSpec((None, d, tf), lambda i, f: (layer, 0, nf + f)),
                  pl.BlockSpec((None, tf, d), lambda i, f: (layer, f, 0)),
                  pl.BlockSpec((tm, d), lambda i, f: (i, 0)),
                  pl.BlockSpec((1, d), lambda i, f: (0, 0)),
                  pl.BlockSpec((1, d), lambda i, f: (0, 0))],
        out_specs=[pl.BlockSpec((tm, d), lambda i, f: (i, 0)),
                   pl.BlockSpec((tm, d), lambda i, f: (i, 0))],
        out_shape=[jax.ShapeDtypeStruct((m, d), F32),
                   jax.ShapeDtypeStruct((m, d), BF16)],
        scratch_shapes=[pltpu.VMEM((tm, d), F32)],
        compiler_params=_params("parallel", "arbitrary"),
        name="ffn",
    )(xb, w_gu, w_gu, w_down, res, g.reshape(1, d), b.reshape(1, d))


ATTN_TQ = 512
ATTN_TK = 512


def kernel(x, mem, rel_bias, w_in, conv_w, gmlp_ln_g, gmlp_ln_b, gmlp_ws, gmlp_bs,
           diff_lambda_q, diff_lambda_k, diff_subln_g, w_mix_out, ln_mix_g, ln_mix_b,
           w_mem_q, w_mem_kv, w_mem_out, ln_mem_g, ln_mem_b,
           w_ffn_gu, w_ffn_down, ln_ffn_g, ln_ffn_b):
    bsz, seq, d = x.shape
    m = bsz * seq
    x = x.reshape(m, d)
    xb = x
    memb = mem.reshape(bsz * N_MEM, d)
    bias_tiles = _near_bias_tiles(rel_bias, ATTN_TQ, ATTN_TK)
    w_mix_out, w_mem_out, w_ffn_gu, w_ffn_down = (
        w.astype(BF16) for w in (w_mix_out, w_mem_out, w_ffn_gu, w_ffn_down))
    for l in range(DEPTH):
        lam_init = 0.8 - 0.6 * math.exp(-0.3 * l)
        tm_in = 1024 if xb.dtype == F32 else 2048
        proj_ab = _matmul(xb, w_in, l, 0, AB_COLS, F32, tm=tm_in, tn=512)
        qkv = _matmul(xb, w_in, l, AB_COLS, QKV_COLS, BF16, tm=tm_in, tn=512)
        y_ab = _mixer_ab(proj_ab, conv_w[l], gmlp_ln_g[l], gmlp_ln_b[l], gmlp_ws[l],
                         gmlp_bs[l], seq, tm=256)
        y_c = _diff_attention(qkv, bias_tiles, diff_lambda_q[l], diff_lambda_k[l],
                              diff_subln_g[l], lam_init, bsz, seq, ATTN_TQ, ATTN_TK)
        x, xb = _matmul_res_ln([y_ab, y_c], w_mix_out, l, x, ln_mix_g[l], ln_mix_b[l], tm=512)

        q_mem = _matmul(xb, w_mem_q, l, 0, D_MODEL, BF16, tm=2048, tn=512)
        kv_mem = _matmul(memb, w_mem_kv, l, 0, 2 * D_MODEL, BF16, tm=bsz * N_MEM, tn=1024)
        o_mem = _mem_attention(q_mem, kv_mem, seq, tm=512)
        x, xb = _matmul_res_ln([o_mem], w_mem_out, l, x, ln_mem_g[l], ln_mem_b[l], tm=512)

        x, xb = _ffn(xb, w_ffn_gu, w_ffn_down, l, x, ln_ffn_g[l], ln_ffn_b[l], tm=512, tf=512)
    return x.reshape(bsz, seq, d)
```

```python
import functools
import math

import jax
import jax.numpy as jnp
from jax import lax
from jax.experimental import pallas as pl
from jax.experimental.pallas import tpu as pltpu

D_MODEL = 2048
DEPTH = 2
N_MEM = 256
MEM_HEADS = 4
MEM_HEAD_DIM = D_MODEL // MEM_HEADS
CONV_WIDTH = D_MODEL // 4
CONV_K = 3
GMLP_WIDTH = D_MODEL // 4
GMLP_GROUPS = 4
GMLP_GROUP_DIM = GMLP_WIDTH // GMLP_GROUPS
CHUNK = 128
DIFF_HEAD_DIM = 128
DIFF_WIDTH = D_MODEL // 2
DIFF_HEADS = DIFF_WIDTH // (2 * DIFF_HEAD_DIM)
AB_COLS = 3 * CONV_WIDTH + 2 * GMLP_WIDTH
QKV_COLS = 3 * DIFF_WIDTH
NUM_BUCKETS = 32
MAX_DISTANCE = 128
D_FF = -(-8 * D_MODEL // (3 * 256)) * 256
ALPHA = (2 * DEPTH) ** 0.25
LN_EPS = 1e-5

V7X_VMEM_BYTES = 64 * 1024 * 1024
V7X_LANES = 128
V7X_SUBLANES = 8
VMEM_LIMIT_BYTES = V7X_VMEM_BYTES - 8 * 1024 * 1024

MASK_VALUE = -1e30
LOG2E = math.log2(math.e)

F32 = jnp.float32
BF16 = jnp.bfloat16


def _params(*semantics):
    return pltpu.CompilerParams(dimension_semantics=semantics,
                                vmem_limit_bytes=VMEM_LIMIT_BYTES)


def _layer_norm_rows(x, g, b):
    mu = jnp.mean(x, axis=-1, keepdims=True)
    xc = x - mu
    var = jnp.mean(xc * xc, axis=-1, keepdims=True)
    return xc * lax.rsqrt(var + LN_EPS) * g + b


def _mm_kernel(a_ref, w_ref, o_ref):
    o_ref[...] = jnp.dot(a_ref[...], w_ref[...],
                         preferred_element_type=F32).astype(o_ref.dtype)


def _matmul(a, w, layer, col0, n, out_dtype, tm, tn):
    m, k = a.shape
    j0 = col0 // tn
    return pl.pallas_call(
        _mm_kernel,
        grid=(m // tm, n // tn),
        in_specs=[pl.BlockSpec((tm, k), lambda i, j: (i, 0)),
                  pl.BlockSpec((None, k, tn), lambda i, j: (layer, 0, j0 + j))],
        out_specs=pl.BlockSpec((tm, tn), lambda i, j: (i, j)),
        out_shape=jax.ShapeDtypeStruct((m, n), out_dtype),
        compiler_params=_params("parallel", "parallel"),
        name="matmul",
    )(a, w)


def _mm_res_ln_kernel(*refs, n_a):
    a_refs = refs[:n_a]
    w_ref, res_ref, g_ref, b_ref, o_ref, ob_ref = refs[n_a:]
    acc = None
    row = 0
    for a_ref in a_refs:
        k = a_ref.shape[1]
        part = jnp.dot(a_ref[...], w_ref[row:row + k, :], preferred_element_type=F32)
        acc = part if acc is None else acc + part
        row += k
    y = _layer_norm_rows(ALPHA * res_ref[...] + acc, g_ref[...], b_ref[...])
    o_ref[...] = y
    ob_ref[...] = y.astype(BF16)


def _matmul_res_ln(a_list, w, layer, res, g, b, tm):
    m = res.shape[0]
    _, k, n = w.shape
    in_specs = [pl.BlockSpec((tm, a.shape[1]), lambda i: (i, 0)) for a in a_list]
    in_specs += [pl.BlockSpec((None, k, n), lambda i: (layer, 0, 0)),
                 pl.BlockSpec((tm, n), lambda i: (i, 0)),
                 pl.BlockSpec((1, n), lambda i: (0, 0)),
                 pl.BlockSpec((1, n), lambda i: (0, 0))]
    return pl.pallas_call(
        functools.partial(_mm_res_ln_kernel, n_a=len(a_list)),
        grid=(m // tm,),
        in_specs=in_specs,
        out_specs=[pl.BlockSpec((tm, n), lambda i: (i, 0)),
                   pl.BlockSpec((tm, n), lambda i: (i, 0))],
        out_shape=[jax.ShapeDtypeStruct((m, n), F32),
                   jax.ShapeDtypeStruct((m, n), BF16)],
        compiler_params=_params("parallel"),
        name="matmul_res_ln",
    )(*a_list, w, res, g.reshape(1, n), b.reshape(1, n))


def _mixer_ab_kernel(bg_ref, cg_ref, h_ref, u_ref, v_ref, cgp_ref, hp_ref,
                     cw_ref, lg_ref, lb_ref, ws_ref, bst_ref, o_ref, *, tm, tiles_per_seq):
    i = pl.program_id(0)
    z = cg_ref[...] * h_ref[...]
    zp = cgp_ref[...] * hp_ref[...]
    zp = jnp.where(i % tiles_per_seq == 0, 0.0, zp)
    rows = lax.broadcasted_iota(jnp.int32, (tm, 1), 0)
    last = zp[V7X_SUBLANES - 1:V7X_SUBLANES, :]
    last2 = zp[V7X_SUBLANES - 2:V7X_SUBLANES - 1, :]
    z1 = jnp.where(rows == 0, last, pltpu.roll(z, 1, axis=0))
    z2 = jnp.where(rows == 0, last2, jnp.where(rows == 1, last, pltpu.roll(z, 2, axis=0)))
    cw = cw_ref[...]
    conv = cw[0:1, :] * z2 + cw[1:2, :] * z1 + cw[2:3, :] * z
    o_ref[:, 0:CONV_WIDTH] = (bg_ref[...] * conv).astype(o_ref.dtype)

    u = jax.nn.gelu(u_ref[...])
    vn = _layer_norm_rows(jax.nn.gelu(v_ref[...]), lg_ref[...], lb_ref[...]).astype(BF16)
    tr = lax.broadcasted_iota(jnp.int32, (CHUNK, CHUNK), 0)
    tc = lax.broadcasted_iota(jnp.int32, (CHUNK, CHUNK), 1)
    causal = tr >= tc
    bst = bst_ref[...]
    for g in range(GMLP_GROUPS):
        wg = jnp.where(causal, ws_ref[g], 0.0).astype(BF16)
        bias = bst[:, g:g + 1]
        c0 = g * GMLP_GROUP_DIM
        for c in range(tm // CHUNK):
            r0 = c * CHUNK
            sv = jnp.dot(wg, vn[r0:r0 + CHUNK, c0:c0 + GMLP_GROUP_DIM],
                         preferred_element_type=F32) + bias
            o_ref[r0:r0 + CHUNK, CONV_WIDTH + c0:CONV_WIDTH + c0 + GMLP_GROUP_DIM] = (
                u[r0:r0 + CHUNK, c0:c0 + GMLP_GROUP_DIM] * sv).astype(o_ref.dtype)


def _mixer_ab(proj_ab, conv_w, ln_g, ln_b, ws, bs, seq, tm):
    m = proj_ab.shape[0]
    w = CONV_WIDTH
    halo_blocks = tm // V7X_SUBLANES

    def col(c):
        return pl.BlockSpec((tm, w), lambda i, c=c: (i, c))

    def halo(c):
        return pl.BlockSpec((V7X_SUBLANES, w),
                            lambda i, c=c: (jnp.maximum(i * halo_blocks - 1, 0), c))

    def whole(shape):
        return pl.BlockSpec(shape, lambda i: (0,) * len(shape))

    return pl.pallas_call(
        functools.partial(_mixer_ab_kernel, tm=tm, tiles_per_seq=seq // tm),
        grid=(m // tm,),
        in_specs=[col(0), col(1), col(2), col(3), col(4), halo(1), halo(2),
                  whole((CONV_K, w)), whole((1, w)), whole((1, w)),
                  whole((GMLP_GROUPS, CHUNK, CHUNK)), whole((CHUNK, GMLP_GROUPS))],
        out_specs=pl.BlockSpec((tm, 2 * w), lambda i: (i, 0)),
        out_shape=jax.ShapeDtypeStruct((m, 2 * w), BF16),
        compiler_params=_params("parallel"),
        name="mixer_ab",
    )(proj_ab, proj_ab, proj_ab, proj_ab, proj_ab, proj_ab, proj_ab,
      conv_w, ln_g.reshape(1, w), ln_b.reshape(1, w), ws, jnp.transpose(bs))


def _causal_bucket(n):
    max_exact = NUM_BUCKETS // 2
    nf = jnp.maximum(n, 1).astype(F32)
    large = max_exact + (jnp.log(nf / max_exact) / math.log(MAX_DISTANCE / max_exact)
                         * (NUM_BUCKETS - max_exact)).astype(jnp.int32)
    large = jnp.minimum(large, NUM_BUCKETS - 1)
    return jnp.where(n < max_exact, n, large)


def _near_bias_tiles(rel_bias, tq, tk):
    r = tq // tk
    qpos = jnp.arange(tq)[None, :]
    kpos = jnp.arange(tk)[:, None]
    dist = jnp.stack([qpos - kpos - e * tk for e in range(-1, r)], axis=0)
    bucket = _causal_bucket(jnp.maximum(dist, 0))
    rb = rel_bias.astype(F32)
    rb = jnp.transpose((rb - rb[NUM_BUCKETS - 1]) * LOG2E).reshape(DIFF_HEADS, 2, NUM_BUCKETS)
    tiles = jnp.zeros((DIFF_HEADS, 2) + dist.shape, F32)
    for b in range(NUM_BUCKETS - 1):
        tiles = jnp.where(bucket == b, rb[:, :, b][:, :, None, None, None], tiles)
    return jnp.where(dist >= 0, tiles, MASK_VALUE)


def _diff_attn_kernel(q_ref, k_ref, v_ref, bias_ref, lq_ref, lk_ref, g_ref, o_ref,
                      acc_ref, m_ref, l_ref, t_ref, pm_ref, qt_ref, *, lam_init, tq, tk):
    qi = pl.program_id(2)
    r = tq // tk
    dh = DIFF_HEAD_DIM
    sub = V7X_SUBLANES
    logit_scale = dh ** -0.5 * LOG2E
    acc_ref[...] = jnp.zeros_like(acc_ref)
    m_ref[...] = jnp.full_like(m_ref, MASK_VALUE)
    l_ref[...] = jnp.zeros_like(l_ref)
    for mp in range(2):
        q = q_ref[:, mp * dh:(mp + 1) * dh].astype(F32)
        qt_ref[mp] = jnp.transpose(q).astype(BF16)

    def fold_keys(x, op):
        out = x[0:sub]
        for i in range(1, x.shape[0] // sub):
            out = op(out, x[i * sub:(i + 1) * sub])
        return out

    def scores(j, slot):
        start = pl.multiple_of(j * tk, tk)
        for mp in range(2):
            k = k_ref[pl.ds(start, tk), mp * dh:(mp + 1) * dh]
            t = jnp.dot(k, qt_ref[mp], preferred_element_type=F32) * logit_scale
            t_ref[slot, mp] = t
            pm_ref[slot, mp] = fold_keys(t, jnp.maximum)

    def accumulate(j, slot, bias_tile):
        start = pl.multiple_of(j * tk, tk)
        v = v_ref[pl.ds(start, tk), :]
        for mp in range(2):
            t = t_ref[slot, mp]
            if bias_tile is None:
                pm = pm_ref[slot, mp]
            else:
                t = t + bias_ref[mp, bias_tile]
                pm = fold_keys(t, jnp.maximum)
            m_prev = m_ref[mp]
            m_new = jnp.maximum(m_prev, jnp.max(pm, axis=0, keepdims=True))
            p = jnp.exp2(t - m_new)
            a = jnp.exp2(m_prev - m_new)
            l_ref[mp] = a * l_ref[mp] + fold_keys(p, jnp.add)
            m_ref[mp] = m_new
            pv = lax.dot_general(v, p.astype(BF16), (((0,), (0,)), ((), ())),
                                 preferred_element_type=F32)
            acc_ref[mp] = a * acc_ref[mp] + pv

    n_far = jnp.maximum(qi * r - 1, 0)
    odd = n_far % 2

    @pl.when(odd == 1)
    def _():
        scores(0, 1)
        accumulate(0, 1, None)

    scores(odd, 0)

    @pl.loop(0, n_far // 2)
    def _(i):
        j = odd + 2 * i
        scores(j + 1, 1)
        accumulate(j, 0, None)
        scores(j + 2, 0)
        accumulate(j + 1, 1, None)

    near = [(qi * r + e, e + 1) for e in range(r)]

    def run_near(blocks):
        for n, (j, tile) in enumerate(blocks):
            if n + 1 < len(blocks):
                scores(blocks[n + 1][0], (n + 1) % 2)
            accumulate(j, n % 2, tile)

    @pl.when(qi > 0)
    def _():
        run_near([(qi * r - 1, 0)] + near)

    @pl.when(qi == 0)
    def _():
        run_near(near)

    outs = []
    for mp in range(2):
        l = jnp.sum(l_ref[mp], axis=0, keepdims=True)
        outs.append(acc_ref[mp] / l)
    sums = jnp.sum(lq_ref[...] * lk_ref[...], axis=-1, keepdims=True)
    ex = jnp.exp(sums)
    lam = ex[0:1, :] - ex[1:2, :] + lam_init
    o = jnp.transpose(outs[0] - lam * outs[1])
    o = o * lax.rsqrt(jnp.mean(o * o, axis=-1, keepdims=True) + LN_EPS) * g_ref[...]
    o_ref[...] = (o * (1.0 - lam_init)).astype(o_ref.dtype)


def _diff_attention(qkv, bias_tiles, lam_q, lam_k, subln_g, lam_init, bsz, seq, tq, tk):
    m = qkv.shape[0]
    hd = 2 * DIFF_HEAD_DIM
    nq = seq // tq
    r = tq // tk
    return pl.pallas_call(
        functools.partial(_diff_attn_kernel, lam_init=lam_init, tq=tq, tk=tk),
        grid=(bsz, DIFF_HEADS, nq),
        in_specs=[
            pl.BlockSpec((tq, hd), lambda b, h, i: (b * nq + i, h)),
            pl.BlockSpec((seq, hd), lambda b, h, i: (b, DIFF_HEADS + h)),
            pl.BlockSpec((seq, hd), lambda b, h, i: (b, 2 * DIFF_HEADS + h)),
            pl.BlockSpec((None, 2, r + 1, tk, tq), lambda b, h, i: (h, 0, 0, 0, 0)),
            pl.BlockSpec((2, DIFF_HEAD_DIM), lambda b, h, i: (0, 0)),
            pl.BlockSpec((2, DIFF_HEAD_DIM), lambda b, h, i: (0, 0)),
            pl.BlockSpec((1, hd), lambda b, h, i: (0, 0)),
        ],
        out_specs=pl.BlockSpec((tq, hd), lambda b, h, i: (b * nq + i, h)),
        out_shape=jax.ShapeDtypeStruct((m, DIFF_WIDTH), BF16),
        scratch_shapes=[pltpu.VMEM((2, hd, tq), F32),
                        pltpu.VMEM((2, 1, tq), F32),
                        pltpu.VMEM((2, V7X_SUBLANES, tq), F32),
                        pltpu.VMEM((2, 2, tk, tq), F32),
                        pltpu.VMEM((2, 2, V7X_SUBLANES, tq), F32),
                        pltpu.VMEM((2, DIFF_HEAD_DIM, tq), BF16)],
        compiler_params=_params("parallel", "parallel", "arbitrary"),
        name="diff_attention",
    )(qkv, qkv, qkv, bias_tiles, lam_q, lam_k, subln_g.reshape(1, hd))


def _mem_attn_kernel(q_ref, kv_ref, o_ref):
    d = MEM_HEAD_DIM
    scale = d ** -0.5
    for h in range(MEM_HEADS):
        q = q_ref[:, h * d:(h + 1) * d]
        k = kv_ref[:, h * d:(h + 1) * d]
        v = kv_ref[:, D_MODEL + h * d:D_MODEL + (h + 1) * d]
        s = lax.dot_general(q, k, (((1,), (1,)), ((), ())), preferred_element_type=F32) * scale
        s = s - jnp.max(s, axis=-1, keepdims=True)
        p = jnp.exp(s)
        p = p / jnp.sum(p, axis=-1, keepdims=True)
        o_ref[:, h * d:(h + 1) * d] = jnp.dot(p.astype(BF16), v,
                                              preferred_element_type=F32).astype(o_ref.dtype)


def _mem_attention(q, kv, seq, tm):
    m = q.shape[0]
    tiles_per_seq = seq // tm
    return pl.pallas_call(
        _mem_attn_kernel,
        grid=(m // tm,),
        in_specs=[pl.BlockSpec((tm, D_MODEL), lambda i: (i, 0)),
                  pl.BlockSpec((N_MEM, 2 * D_MODEL), lambda i: (i // tiles_per_seq, 0))],
        out_specs=pl.BlockSpec((tm, D_MODEL), lambda i: (i, 0)),
        out_shape=jax.ShapeDtypeStruct((m, D_MODEL), BF16),
        compiler_params=_params("parallel"),
        name="mem_attention",
    )(q, kv)


def _ffn_kernel(xb_ref, wg_ref, wu_ref, wd_ref, res_ref, g_ref, b_ref, o_ref, ob_ref, acc_ref):
    f = pl.program_id(1)

    @pl.when(f == 0)
    def _():
        acc_ref[...] = jnp.zeros_like(acc_ref)

    xb = xb_ref[...]
    gate = jnp.dot(xb, wg_ref[...], preferred_element_type=F32)
    up = jnp.dot(xb, wu_ref[...], preferred_element_type=F32)
    hid = (gate * (1.0 / (1.0 + jnp.exp(-gate))) * up).astype(BF16)
    acc_ref[...] += jnp.dot(hid, wd_ref[...], preferred_element_type=F32)

    @pl.when(f == pl.num_programs(1) - 1)
    def _():
        y = _layer_norm_rows(ALPHA * res_ref[...] + acc_ref[...], g_ref[...], b_ref[...])
        o_ref[...] = y
        ob_ref[...] = y.astype(BF16)


def _ffn(xb, w_gu, w_down, layer, res, g, b, tm, tf):
    m, d = xb.shape
    nf = D_FF // tf
    return pl.pallas_call(
        _ffn_kernel,
        grid=(m // tm, nf),
        in_specs=[pl.BlockSpec((tm, d), lambda i, f: (i, 0)),
                  pl.BlockSpec((None, d, tf), lambda i, f: (layer, 0, f)),
                  pl.BlockSpec((None, d, tf), lambda i, f: (layer, 0, nf + f)),
                  pl.BlockSpec((None, tf, d), lambda i, f: (layer, f, 0)),
                  pl.BlockSpec((tm, d), lambda i, f: (i, 0)),
                  pl.BlockSpec((1, d), lambda i, f: (0, 0)),
                  pl.BlockSpec((1, d), lambda i, f: (0, 0))],
        out_specs=[pl.BlockSpec((tm, d), lambda i, f: (i, 0)),
                   pl.BlockSpec((tm, d), lambda i, f: (i, 0))],
        out_shape=[jax.ShapeDtypeStruct((m, d), F32),
                   jax.ShapeDtypeStruct((m, d), BF16)],
        scratch_shapes=[pltpu.VMEM((tm, d), F32)],
        compiler_params=_params("parallel", "arbitrary"),
        name="ffn",
    )(xb, w_gu, w_gu, w_down, res, g.reshape(1, d), b.reshape(1, d))


ATTN_TQ = 512
ATTN_TK = 512


def kernel(x, mem, rel_bias, w_in, conv_w, gmlp_ln_g, gmlp_ln_b, gmlp_ws, gmlp_bs,
           diff_lambda_q, diff_lambda_k, diff_subln_g, w_mix_out, ln_mix_g, ln_mix_b,
           w_mem_q, w_mem_kv, w_mem_out, ln_mem_g, ln_mem_b,
           w_ffn_gu, w_ffn_down, ln_ffn_g, ln_ffn_b):
    bsz, seq, d = x.shape
    m = bsz * seq
    x = x.reshape(m, d)
    xb = x.astype(BF16)
    memb = mem.reshape(bsz * N_MEM, d).astype(BF16)
    bias_tiles = _near_bias_tiles(rel_bias, ATTN_TQ, ATTN_TK)
    w_in, w_mix_out, w_mem_q, w_mem_kv, w_mem_out, w_ffn_gu, w_ffn_down = (
        w.astype(BF16) for w in (w_in, w_mix_out, w_mem_q, w_mem_kv, w_mem_out,
                                 w_ffn_gu, w_ffn_down))
    for l in range(DEPTH):
        lam_init = 0.8 - 0.6 * math.exp(-0.3 * l)
        proj_ab = _matmul(xb, w_in, l, 0, AB_COLS, F32, tm=2048, tn=512)
        qkv = _matmul(xb, w_in, l, AB_COLS, QKV_COLS, BF16, tm=2048, tn=512)
        y_ab = _mixer_ab(proj_ab, conv_w[l], gmlp_ln_g[l], gmlp_ln_b[l], gmlp_ws[l],
                         gmlp_bs[l], seq, tm=256)
        y_c = _diff_attention(qkv, bias_tiles, diff_lambda_q[l], diff_lambda_k[l],
                              diff_subln_g[l], lam_init, bsz, seq, ATTN_TQ, ATTN_TK)
        x, xb = _matmul_res_ln([y_ab, y_c], w_mix_out, l, x, ln_mix_g[l], ln_mix_b[l], tm=512)

        q_mem = _matmul(xb, w_mem_q, l, 0, D_MODEL, BF16, tm=2048, tn=1024)
        kv_mem = _matmul(memb, w_mem_kv, l, 0, 2 * D_MODEL, BF16, tm=bsz * N_MEM, tn=1024)
        o_mem = _mem_attention(q_mem, kv_mem, seq, tm=512)
        x, xb = _matmul_res_ln([o_mem], w_mem_out, l, x, ln_mem_g[l], ln_mem_b[l], tm=512)

        x, xb = _ffn(xb, w_ffn_gu, w_ffn_down, l, x, ln_ffn_g[l], ln_ffn_b[l], tm=512, tf=512)
    return x.reshape(bsz, seq, d)
```

```python
import functools
import math

import jax
import jax.numpy as jnp
from jax import lax
from jax.experimental import pallas as pl
from jax.experimental.pallas import tpu as pltpu

D_MODEL = 2048
DEPTH = 2
N_MEM = 256
MEM_HEADS = 4
MEM_HEAD_DIM = D_MODEL // MEM_HEADS
CONV_WIDTH = D_MODEL // 4
CONV_K = 3
GMLP_WIDTH = D_MODEL // 4
GMLP_GROUPS = 4
GMLP_GROUP_DIM = GMLP_WIDTH // GMLP_GROUPS
CHUNK = 128
DIFF_HEAD_DIM = 128
DIFF_WIDTH = D_MODEL // 2
DIFF_HEADS = DIFF_WIDTH // (2 * DIFF_HEAD_DIM)
AB_COLS = 3 * CONV_WIDTH + 2 * GMLP_WIDTH
QKV_COLS = 3 * DIFF_WIDTH
NUM_BUCKETS = 32
MAX_DISTANCE = 128
D_FF = -(-8 * D_MODEL // (3 * 256)) * 256
ALPHA = (2 * DEPTH) ** 0.25
LN_EPS = 1e-5

V7X_VMEM_BYTES = 64 * 1024 * 1024
V7X_LANES = 128
V7X_SUBLANES = 8
VMEM_LIMIT_BYTES = V7X_VMEM_BYTES - 8 * 1024 * 1024

MASK_VALUE = -1e30
LOG2E = math.log2(math.e)

F32 = jnp.float32
BF16 = jnp.bfloat16


def _params(*semantics):
    return pltpu.CompilerParams(dimension_semantics=semantics,
                                vmem_limit_bytes=VMEM_LIMIT_BYTES)


def _layer_norm_rows(x, g, b):
    mu = jnp.mean(x, axis=-1, keepdims=True)
    xc = x - mu
    var = jnp.mean(xc * xc, axis=-1, keepdims=True)
    return xc * lax.rsqrt(var + LN_EPS) * g + b


def _mm_kernel(a_ref, w_ref, o_ref):
    o_ref[...] = jnp.dot(a_ref[...], w_ref[...],
                         preferred_element_type=F32).astype(o_ref.dtype)


def _matmul(a, w, layer, col0, n, out_dtype, tm, tn):
    m, k = a.shape
    j0 = col0 // tn
    return pl.pallas_call(
        _mm_kernel,
        grid=(m // tm, n // tn),
        in_specs=[pl.BlockSpec((tm, k), lambda i, j: (i, 0)),
                  pl.BlockSpec((None, k, tn), lambda i, j: (layer, 0, j0 + j))],
        out_specs=pl.BlockSpec((tm, tn), lambda i, j: (i, j)),
        out_shape=jax.ShapeDtypeStruct((m, n), out_dtype),
        compiler_params=_params("parallel", "parallel"),
        name="matmul",
    )(a, w)


def _mm_res_ln_kernel(*refs, n_a):
    a_refs = refs[:n_a]
    w_ref, res_ref, g_ref, b_ref, o_ref, ob_ref = refs[n_a:]
    acc = None
    row = 0
    for a_ref in a_refs:
        k = a_ref.shape[1]
        part = jnp.dot(a_ref[...], w_ref[row:row + k, :], preferred_element_type=F32)
        acc = part if acc is None else acc + part
        row += k
    y = _layer_norm_rows(ALPHA * res_ref[...] + acc, g_ref[...], b_ref[...])
    o_ref[...] = y
    ob_ref[...] = y.astype(BF16)


def _matmul_res_ln(a_list, w, layer, res, g, b, tm):
    m = res.shape[0]
    _, k, n = w.shape
    in_specs = [pl.BlockSpec((tm, a.shape[1]), lambda i: (i, 0)) for a in a_list]
    in_specs += [pl.BlockSpec((None, k, n), lambda i: (layer, 0, 0)),
                 pl.BlockSpec((tm, n), lambda i: (i, 0)),
                 pl.BlockSpec((None, 1, n), lambda i: (layer, 0, 0)),
                 pl.BlockSpec((None, 1, n), lambda i: (layer, 0, 0))]
    return pl.pallas_call(
        functools.partial(_mm_res_ln_kernel, n_a=len(a_list)),
        grid=(m // tm,),
        in_specs=in_specs,
        out_specs=[pl.BlockSpec((tm, n), lambda i: (i, 0)),
                   pl.BlockSpec((tm, n), lambda i: (i, 0))],
        out_shape=[jax.ShapeDtypeStruct((m, n), F32),
                   jax.ShapeDtypeStruct((m, n), BF16)],
        compiler_params=_params("parallel"),
        name="matmul_res_ln",
    )(*a_list, w, res, g.reshape(-1, 1, n), b.reshape(-1, 1, n))


def _mixer_ab_kernel(bg_ref, cg_ref, h_ref, u_ref, v_ref, cgp_ref, hp_ref,
                     cw_ref, lg_ref, lb_ref, ws_ref, bst_ref, o_ref, *, tm, tiles_per_seq):
    i = pl.program_id(0)
    z = cg_ref[...] * h_ref[...]
    zp = cgp_ref[...] * hp_ref[...]
    zp = jnp.where(i % tiles_per_seq == 0, 0.0, zp)
    rows = lax.broadcasted_iota(jnp.int32, (tm, 1), 0)
    last = zp[V7X_SUBLANES - 1:V7X_SUBLANES, :]
    last2 = zp[V7X_SUBLANES - 2:V7X_SUBLANES - 1, :]
    z1 = jnp.where(rows == 0, last, pltpu.roll(z, 1, axis=0))
    z2 = jnp.where(rows == 0, last2, jnp.where(rows == 1, last, pltpu.roll(z, 2, axis=0)))
    cw = cw_ref[...]
    conv = cw[0:1, :] * z2 + cw[1:2, :] * z1 + cw[2:3, :] * z
    o_ref[:, 0:CONV_WIDTH] = (bg_ref[...] * conv).astype(o_ref.dtype)

    u = jax.nn.gelu(u_ref[...])
    vn = _layer_norm_rows(jax.nn.gelu(v_ref[...]), lg_ref[...], lb_ref[...]).astype(BF16)
    tr = lax.broadcasted_iota(jnp.int32, (CHUNK, CHUNK), 0)
    tc = lax.broadcasted_iota(jnp.int32, (CHUNK, CHUNK), 1)
    causal = tr >= tc
    bst = bst_ref[...]
    for g in range(GMLP_GROUPS):
        wg = jnp.where(causal, ws_ref[g], 0.0).astype(BF16)
        bias = bst[:, g:g + 1]
        c0 = g * GMLP_GROUP_DIM
        for c in range(tm // CHUNK):
            r0 = c * CHUNK
            sv = jnp.dot(wg, vn[r0:r0 + CHUNK, c0:c0 + GMLP_GROUP_DIM],
                         preferred_element_type=F32) + bias
            o_ref[r0:r0 + CHUNK, CONV_WIDTH + c0:CONV_WIDTH + c0 + GMLP_GROUP_DIM] = (
                u[r0:r0 + CHUNK, c0:c0 + GMLP_GROUP_DIM] * sv).astype(o_ref.dtype)


def _mixer_ab(proj_ab, conv_w, ln_g, ln_b, ws, bs_t, layer, seq, tm):
    m = proj_ab.shape[0]
    w = CONV_WIDTH
    halo_blocks = tm // V7X_SUBLANES

    def col(c):
        return pl.BlockSpec((tm, w), lambda i, c=c: (i, c))

    def halo(c):
        return pl.BlockSpec((V7X_SUBLANES, w),
                            lambda i, c=c: (jnp.maximum(i * halo_blocks - 1, 0), c))

    def whole(shape):
        return pl.BlockSpec((None,) + shape, lambda i: (layer,) + (0,) * len(shape))

    return pl.pallas_call(
        functools.partial(_mixer_ab_kernel, tm=tm, tiles_per_seq=seq // tm),
        grid=(m // tm,),
        in_specs=[col(0), col(1), col(2), col(3), col(4), halo(1), halo(2),
                  whole((CONV_K, w)), whole((1, w)), whole((1, w)),
                  whole((GMLP_GROUPS, CHUNK, CHUNK)), whole((CHUNK, GMLP_GROUPS))],
        out_specs=pl.BlockSpec((tm, 2 * w), lambda i: (i, 0)),
        out_shape=jax.ShapeDtypeStruct((m, 2 * w), BF16),
        compiler_params=_params("parallel"),
        name="mixer_ab",
    )(proj_ab, proj_ab, proj_ab, proj_ab, proj_ab, proj_ab, proj_ab,
      conv_w, ln_g.reshape(-1, 1, w), ln_b.reshape(-1, 1, w), ws, bs_t)


def _causal_bucket(n):
    max_exact = NUM_BUCKETS // 2
    nf = jnp.maximum(n, 1).astype(F32)
    large = max_exact + (jnp.log(nf / max_exact) / math.log(MAX_DISTANCE / max_exact)
                         * (NUM_BUCKETS - max_exact)).astype(jnp.int32)
    large = jnp.minimum(large, NUM_BUCKETS - 1)
    return jnp.where(n < max_exact, n, large)


def _near_bias_tiles(rel_bias, tq, tk):
    r = tq // tk
    rb = rel_bias.astype(F32)
    rb = jnp.transpose((rb - rb[NUM_BUCKETS - 1]) * LOG2E).reshape(DIFF_HEADS, 2, NUM_BUCKETS)

    def lookup(dist):
        bucket = _causal_bucket(jnp.maximum(dist, 0))
        tiles = jnp.zeros((DIFF_HEADS, 2) + dist.shape, F32)
        for b in range(NUM_BUCKETS - 1):
            tiles = jnp.where(bucket == b, rb[:, :, b].reshape((DIFF_HEADS, 2) + (1,) * dist.ndim),
                              tiles)
        return jnp.where(dist >= 0, tiles, MASK_VALUE)

    qpos = jnp.arange(tq)[None, :]
    kpos = jnp.arange(tk)[:, None]
    diag = lookup(jnp.stack([qpos - kpos - e * tk for e in range(r)], axis=0))
    c = MAX_DISTANCE
    corner = lookup(jnp.arange(c)[None, :] - jnp.arange(c)[:, None] + c)
    return diag, corner


def _diff_attn_kernel(q_ref, k_ref, v_ref, bias_ref, corner_ref, lq_ref, lk_ref, g_ref, o_ref,
                      acc_ref, m_ref, l_ref, t_ref, pm_ref, qt_ref, *, lam_init, tq, tk):
    qi = pl.program_id(2)
    r = tq // tk
    dh = DIFF_HEAD_DIM
    sub = V7X_SUBLANES
    logit_scale = dh ** -0.5 * LOG2E
    acc_ref[...] = jnp.zeros_like(acc_ref)
    m_ref[...] = jnp.full_like(m_ref, MASK_VALUE)
    l_ref[...] = jnp.zeros_like(l_ref)
    for mp in range(2):
        q = q_ref[:, mp * dh:(mp + 1) * dh].astype(F32)
        qt_ref[mp] = jnp.transpose(q).astype(BF16)

    def fold_keys(x, op):
        out = x[0:sub]
        for i in range(1, x.shape[0] // sub):
            out = op(out, x[i * sub:(i + 1) * sub])
        return out

    def scores(j, slot):
        start = pl.multiple_of(j * tk, tk)
        is_below = (j == qi * r - 1).astype(F32)
        c = MAX_DISTANCE
        for mp in range(2):
            k = k_ref[pl.ds(start, tk), mp * dh:(mp + 1) * dh]
            t = jnp.dot(k, qt_ref[mp], preferred_element_type=F32) * logit_scale
            head, tail = t[0:tk - c], t[tk - c:tk]
            tail = jnp.concatenate([tail[:, 0:c] + is_below * corner_ref[mp], tail[:, c:]], axis=1)
            t_ref[slot, mp, 0:tk - c, :] = head
            t_ref[slot, mp, tk - c:tk, :] = tail
            pm_ref[slot, mp] = jnp.maximum(fold_keys(head, jnp.maximum),
                                           fold_keys(tail, jnp.maximum))

    def accumulate(j, slot, bias_tile):
        start = pl.multiple_of(j * tk, tk)
        v = v_ref[pl.ds(start, tk), :]
        for mp in range(2):
            t = t_ref[slot, mp]
            if bias_tile is None:
                pm = pm_ref[slot, mp]
            else:
                t = t + bias_ref[mp, bias_tile]
                pm = fold_keys(t, jnp.maximum)
            m_prev = m_ref[mp]
            m_new = jnp.maximum(m_prev, jnp.max(pm, axis=0, keepdims=True))
            p = jnp.exp2(t - m_new)
            a = jnp.exp2(m_prev - m_new)
            l_ref[mp] = a * l_ref[mp] + fold_keys(p, jnp.add)
            m_ref[mp] = m_new
            pv = lax.dot_general(v, p.astype(BF16), (((0,), (0,)), ((), ())),
                                 preferred_element_type=F32)
            acc_ref[mp] = a * acc_ref[mp] + pv

    n_below = qi * r
    odd = n_below % 2

    @pl.when(odd == 1)
    def _():
        scores(0, 1)
        accumulate(0, 1, None)

    scores(odd, 0)

    @pl.loop(0, n_below // 2)
    def _(i):
        j = odd + 2 * i
        scores(j + 1, 1)
        accumulate(j, 0, None)
        scores(j + 2, 0)
        accumulate(j + 1, 1, None)

    for e in range(r):
        if e + 1 < r:
            scores(n_below + e + 1, (e + 1) % 2)
        accumulate(n_below + e, e % 2, e)

    outs = []
    for mp in range(2):
        l = jnp.sum(l_ref[mp], axis=0, keepdims=True)
        outs.append(acc_ref[mp] / l)
    sums = jnp.sum(lq_ref[...] * lk_ref[...], axis=-1, keepdims=True)
    ex = jnp.exp(sums)
    lam = ex[0:1, :] - ex[1:2, :] + lam_init
    o = jnp.transpose(outs[0] - lam * outs[1])
    o = o * lax.rsqrt(jnp.mean(o * o, axis=-1, keepdims=True) + LN_EPS) * g_ref[...]
    o_ref[...] = (o * (1.0 - lam_init)).astype(o_ref.dtype)


def _diff_attention(qkv, bias_tiles, lam_q, lam_k, subln_g, layer, lam_init, bsz, seq, tq, tk):
    m = qkv.shape[0]
    hd = 2 * DIFF_HEAD_DIM
    nq = seq // tq
    r = tq // tk
    diag, corner = bias_tiles
    c = MAX_DISTANCE
    return pl.pallas_call(
        functools.partial(_diff_attn_kernel, lam_init=lam_init, tq=tq, tk=tk),
        grid=(bsz, DIFF_HEADS, nq),
        in_specs=[
            pl.BlockSpec((tq, hd), lambda b, h, i: (b * nq + i, h)),
            pl.BlockSpec((seq, hd), lambda b, h, i: (b, DIFF_HEADS + h)),
            pl.BlockSpec((seq, hd), lambda b, h, i: (b, 2 * DIFF_HEADS + h)),
            pl.BlockSpec((None, 2, r, tk, tq), lambda b, h, i: (h, 0, 0, 0, 0)),
            pl.BlockSpec((None, 2, c, c), lambda b, h, i: (h, 0, 0, 0)),
            pl.BlockSpec((None, 2, DIFF_HEAD_DIM), lambda b, h, i: (layer, 0, 0)),
            pl.BlockSpec((None, 2, DIFF_HEAD_DIM), lambda b, h, i: (layer, 0, 0)),
            pl.BlockSpec((None, 1, hd), lambda b, h, i: (layer, 0, 0)),
        ],
        out_specs=pl.BlockSpec((tq, hd), lambda b, h, i: (b * nq + i, h)),
        out_shape=jax.ShapeDtypeStruct((m, DIFF_WIDTH), BF16),
        scratch_shapes=[pltpu.VMEM((2, hd, tq), F32),
                        pltpu.VMEM((2, 1, tq), F32),
                        pltpu.VMEM((2, V7X_SUBLANES, tq), F32),
                        pltpu.VMEM((2, 2, tk, tq), F32),
                        pltpu.VMEM((2, 2, V7X_SUBLANES, tq), F32),
                        pltpu.VMEM((2, DIFF_HEAD_DIM, tq), BF16)],
        compiler_params=_params("parallel", "parallel", "arbitrary"),
        name="diff_attention",
    )(qkv, qkv, qkv, diag, corner, lam_q, lam_k, subln_g.reshape(-1, 1, hd))


def _mem_attn_kernel(q_ref, kv_ref, o_ref):
    d = MEM_HEAD_DIM
    scale = d ** -0.5
    for h in range(MEM_HEADS):
        q = q_ref[:, h * d:(h + 1) * d]
        k = kv_ref[:, h * d:(h + 1) * d]
        v = kv_ref[:, D_MODEL + h * d:D_MODEL + (h + 1) * d]
        s = lax.dot_general(q, k, (((1,), (1,)), ((), ())), preferred_element_type=F32) * scale
        s = s - jnp.max(s, axis=-1, keepdims=True)
        p = jnp.exp(s)
        p = p / jnp.sum(p, axis=-1, keepdims=True)
        o_ref[:, h * d:(h + 1) * d] = jnp.dot(p.astype(BF16), v,
                                              preferred_element_type=F32).astype(o_ref.dtype)


def _mem_attention(q, kv, seq, tm):
    m = q.shape[0]
    tiles_per_seq = seq // tm
    return pl.pallas_call(
        _mem_attn_kernel,
        grid=(m // tm,),
        in_specs=[pl.BlockSpec((tm, D_MODEL), lambda i: (i, 0)),
                  pl.BlockSpec((N_MEM, 2 * D_MODEL), lambda i: (i // tiles_per_seq, 0))],
        out_specs=pl.BlockSpec((tm, D_MODEL), lambda i: (i, 0)),
        out_shape=jax.ShapeDtypeStruct((m, D_MODEL), BF16),
        compiler_params=_params("parallel"),
        name="mem_attention",
    )(q, kv)


def _ffn_kernel(xb_ref, wg_ref, wu_ref, wd_ref, res_ref, g_ref, b_ref, o_ref, ob_ref, acc_ref):
    f = pl.program_id(1)

    @pl.when(f == 0)
    def _():
        acc_ref[...] = jnp.zeros_like(acc_ref)

    xb = xb_ref[...]
    gate = jnp.dot(xb, wg_ref[...], preferred_element_type=F32)
    up = jnp.dot(xb, wu_ref[...], preferred_element_type=F32)
    hid = (gate * (1.0 / (1.0 + jnp.exp(-gate))) * up).astype(BF16)
    acc_ref[...] += jnp.dot(hid, wd_ref[...], preferred_element_type=F32)

    @pl.when(f == pl.num_programs(1) - 1)
    def _():
        y = _layer_norm_rows(ALPHA * res_ref[...] + acc_ref[...], g_ref[...], b_ref[...])
        o_ref[...] = y
        ob_ref[...] = y.astype(BF16)


def _ffn(xb, w_gu, w_down, layer, res, g, b, tm, tf):
    m, d = xb.shape
    nf = D_FF // tf
    return pl.pallas_call(
        _ffn_kernel,
        grid=(m // tm, nf),
        in_specs=[pl.BlockSpec((tm, d), lambda i, f: (i, 0)),
                  pl.BlockSpec((None, d, tf), lambda i, f: (layer, 0, f)),
                  pl.BlockSpec((None, d, tf), lambda i, f: (layer, 0, nf + f)),
                  pl.BlockSpec((None, tf, d), lambda i, f: (layer, f, 0)),
                  pl.BlockSpec((tm, d), lambda i, f: (i, 0)),
                  pl.BlockSpec((None, 1, d), lambda i, f: (layer, 0, 0)),
                  pl.BlockSpec((None, 1, d), lambda i, f: (layer, 0, 0))],
        out_specs=[pl.BlockSpec((tm, d), lambda i, f: (i, 0)),
                   pl.BlockSpec((tm, d), lambda i, f: (i, 0))],
        out_shape=[jax.ShapeDtypeStruct((m, d), F32),
                   jax.ShapeDtypeStruct((m, d), BF16)],
        scratch_shapes=[pltpu.VMEM((tm, d), F32)],
        compiler_params=_params("parallel", "arbitrary"),
        name="ffn",
    )(xb, w_gu, w_gu, w_down, res, g.reshape(-1, 1, d), b.reshape(-1, 1, d))


ATTN_TQ = 512
ATTN_TK = 512


def kernel(x, mem, rel_bias, w_in, conv_w, gmlp_ln_g, gmlp_ln_b, gmlp_ws, gmlp_bs,
           diff_lambda_q, diff_lambda_k, diff_subln_g, w_mix_out, ln_mix_g, ln_mix_b,
           w_mem_q, w_mem_kv, w_mem_out, ln_mem_g, ln_mem_b,
           w_ffn_gu, w_ffn_down, ln_ffn_g, ln_ffn_b):
    bsz, seq, d = x.shape
    m = bsz * seq
    x = x.reshape(m, d)
    xb = x.astype(BF16)
    memb = mem.reshape(bsz * N_MEM, d).astype(BF16)
    bias_tiles = _near_bias_tiles(rel_bias, ATTN_TQ, ATTN_TK)
    w_in, w_mix_out, w_mem_q, w_mem_kv, w_mem_out, w_ffn_gu, w_ffn_down = (
        w.astype(BF16) for w in (w_in, w_mix_out, w_mem_q, w_mem_kv, w_mem_out,
                                 w_ffn_gu, w_ffn_down))
    gmlp_bs_t = jnp.transpose(gmlp_bs, (0, 2, 1))
    for l in range(DEPTH):
        lam_init = 0.8 - 0.6 * math.exp(-0.3 * l)
        proj_ab = _matmul(xb, w_in, l, 0, AB_COLS, F32, tm=2048, tn=512)
        qkv = _matmul(xb, w_in, l, AB_COLS, QKV_COLS, BF16, tm=2048, tn=512)
        y_ab = _mixer_ab(proj_ab, conv_w, gmlp_ln_g, gmlp_ln_b, gmlp_ws, gmlp_bs_t, l, seq, tm=256)
        y_c = _diff_attention(qkv, bias_tiles, diff_lambda_q, diff_lambda_k, diff_subln_g, l,
                              lam_init, bsz, seq, ATTN_TQ, ATTN_TK)
        x, xb = _matmul_res_ln([y_ab, y_c], w_mix_out, l, x, ln_mix_g, ln_mix_b, tm=512)

        q_mem = _matmul(xb, w_mem_q, l, 0, D_MODEL, BF16, tm=2048, tn=1024)
        kv_mem = _matmul(memb, w_mem_kv, l, 0, 2 * D_MODEL, BF16, tm=bsz * N_MEM, tn=1024)
        o_mem = _mem_attention(q_mem, kv_mem, seq, tm=512)
        x, xb = _matmul_res_ln([o_mem], w_mem_out, l, x, ln_mem_g, ln_mem_b, tm=512)

        x, xb = _ffn(xb, w_ffn_gu, w_ffn_down, l, x, ln_ffn_g, ln_ffn_b, tm=512, tf=512)
    return x.reshape(bsz, seq, d)
```

```python
import functools
import math

import jax
import jax.numpy as jnp
from jax import lax
from jax.experimental import pallas as pl
from jax.experimental.pallas import tpu as pltpu

D_MODEL = 2048
DEPTH = 2
N_MEM = 256
MEM_HEADS = 4
MEM_HEAD_DIM = D_MODEL // MEM_HEADS
CONV_WIDTH = D_MODEL // 4
CONV_K = 3
GMLP_WIDTH = D_MODEL // 4
GMLP_GROUPS = 4
GMLP_GROUP_DIM = GMLP_WIDTH // GMLP_GROUPS
CHUNK = 128
DIFF_HEAD_DIM = 128
DIFF_WIDTH = D_MODEL // 2
DIFF_HEADS = DIFF_WIDTH // (2 * DIFF_HEAD_DIM)
AB_COLS = 3 * CONV_WIDTH + 2 * GMLP_WIDTH
QKV_COLS = 3 * DIFF_WIDTH
NUM_BUCKETS = 32
MAX_DISTANCE = 128
D_FF = -(-8 * D_MODEL // (3 * 256)) * 256
ALPHA = (2 * DEPTH) ** 0.25
LN_EPS = 1e-5

V7X_VMEM_BYTES = 64 * 1024 * 1024
V7X_LANES = 128
V7X_SUBLANES = 8
VMEM_LIMIT_BYTES = V7X_VMEM_BYTES - 8 * 1024 * 1024

MASK_VALUE = -1e30
LOG2E = math.log2(math.e)

F32 = jnp.float32
BF16 = jnp.bfloat16


def _params(*semantics):
    return pltpu.CompilerParams(dimension_semantics=semantics,
                                vmem_limit_bytes=VMEM_LIMIT_BYTES)


def _layer_norm_rows(x, g, b):
    mu = jnp.mean(x, axis=-1, keepdims=True)
    xc = x - mu
    var = jnp.mean(xc * xc, axis=-1, keepdims=True)
    return xc * lax.rsqrt(var + LN_EPS) * g + b


def _mm_kernel(a_ref, w_ref, o_ref):
    o_ref[...] = jnp.dot(a_ref[...], w_ref[...],
                         preferred_element_type=F32).astype(o_ref.dtype)


def _matmul(a, w, layer, col0, n, out_dtype, tm, tn):
    m, k = a.shape
    j0 = col0 // tn
    return pl.pallas_call(
        _mm_kernel,
        grid=(m // tm, n // tn),
        in_specs=[pl.BlockSpec((tm, k), lambda i, j: (i, 0)),
                  pl.BlockSpec((None, k, tn), lambda i, j: (layer, 0, j0 + j))],
        out_specs=pl.BlockSpec((tm, tn), lambda i, j: (i, j)),
        out_shape=jax.ShapeDtypeStruct((m, n), out_dtype),
        compiler_params=_params("parallel", "parallel"),
        name="matmul",
    )(a, w)


def _mm_res_ln_kernel(*refs, n_a):
    a_refs = refs[:n_a]
    w_ref, res_ref, g_ref, b_ref, o_ref, ob_ref = refs[n_a:]
    acc = None
    row = 0
    for a_ref in a_refs:
        k = a_ref.shape[1]
        part = jnp.dot(a_ref[...], w_ref[row:row + k, :], preferred_element_type=F32)
        acc = part if acc is None else acc + part
        row += k
    y = _layer_norm_rows(ALPHA * res_ref[...] + acc, g_ref[...], b_ref[...])
    o_ref[...] = y
    ob_ref[...] = y.astype(BF16)


def _matmul_res_ln(a_list, w, layer, res, g, b, tm):
    m = res.shape[0]
    _, k, n = w.shape
    in_specs = [pl.BlockSpec((tm, a.shape[1]), lambda i: (i, 0)) for a in a_list]
    in_specs += [pl.BlockSpec((None, k, n), lambda i: (layer, 0, 0)),
                 pl.BlockSpec((tm, n), lambda i: (i, 0)),
                 pl.BlockSpec((None, 1, n), lambda i: (layer, 0, 0)),
                 pl.BlockSpec((None, 1, n), lambda i: (layer, 0, 0))]
    return pl.pallas_call(
        functools.partial(_mm_res_ln_kernel, n_a=len(a_list)),
        grid=(m // tm,),
        in_specs=in_specs,
        out_specs=[pl.BlockSpec((tm, n), lambda i: (i, 0)),
                   pl.BlockSpec((tm, n), lambda i: (i, 0))],
        out_shape=[jax.ShapeDtypeStruct((m, n), F32),
                   jax.ShapeDtypeStruct((m, n), BF16)],
        compiler_params=_params("parallel"),
        name="matmul_res_ln",
    )(*a_list, w, res, g.reshape(-1, 1, n), b.reshape(-1, 1, n))


def _mixer_ab_kernel(bg_ref, cg_ref, h_ref, u_ref, v_ref, cgp_ref, hp_ref,
                     cw_ref, lg_ref, lb_ref, ws_ref, bst_ref, o_ref, *, tm, tiles_per_seq):
    i = pl.program_id(0)
    z = cg_ref[...] * h_ref[...]
    zp = cgp_ref[...] * hp_ref[...]
    zp = jnp.where(i % tiles_per_seq == 0, 0.0, zp)
    rows = lax.broadcasted_iota(jnp.int32, (tm, 1), 0)
    last = zp[V7X_SUBLANES - 1:V7X_SUBLANES, :]
    last2 = zp[V7X_SUBLANES - 2:V7X_SUBLANES - 1, :]
    z1 = jnp.where(rows == 0, last, pltpu.roll(z, 1, axis=0))
    z2 = jnp.where(rows == 0, last2, jnp.where(rows == 1, last, pltpu.roll(z, 2, axis=0)))
    cw = cw_ref[...]
    conv = cw[0:1, :] * z2 + cw[1:2, :] * z1 + cw[2:3, :] * z
    o_ref[:, 0:CONV_WIDTH] = (bg_ref[...] * conv).astype(o_ref.dtype)

    u = jax.nn.gelu(u_ref[...])
    vn = _layer_norm_rows(jax.nn.gelu(v_ref[...]), lg_ref[...], lb_ref[...]).astype(BF16)
    tr = lax.broadcasted_iota(jnp.int32, (CHUNK, CHUNK), 0)
    tc = lax.broadcasted_iota(jnp.int32, (CHUNK, CHUNK), 1)
    causal = tr >= tc
    bst = bst_ref[...]
    for g in range(GMLP_GROUPS):
        wg = jnp.where(causal, ws_ref[g], 0.0).astype(BF16)
        bias = bst[:, g:g + 1]
        c0 = g * GMLP_GROUP_DIM
        for c in range(tm // CHUNK):
            r0 = c * CHUNK
            sv = jnp.dot(wg, vn[r0:r0 + CHUNK, c0:c0 + GMLP_GROUP_DIM],
                         preferred_element_type=F32) + bias
            o_ref[r0:r0 + CHUNK, CONV_WIDTH + c0:CONV_WIDTH + c0 + GMLP_GROUP_DIM] = (
                u[r0:r0 + CHUNK, c0:c0 + GMLP_GROUP_DIM] * sv).astype(o_ref.dtype)


def _mixer_ab(proj_ab, conv_w, ln_g, ln_b, ws, bs_t, layer, seq, tm):
    m = proj_ab.shape[0]
    w = CONV_WIDTH
    halo_blocks = tm // V7X_SUBLANES

    def col(c):
        return pl.BlockSpec((tm, w), lambda i, c=c: (i, c))

    def halo(c):
        return pl.BlockSpec((V7X_SUBLANES, w),
                            lambda i, c=c: (jnp.maximum(i * halo_blocks - 1, 0), c))

    def whole(shape):
        return pl.BlockSpec((None,) + shape, lambda i: (layer,) + (0,) * len(shape))

    return pl.pallas_call(
        functools.partial(_mixer_ab_kernel, tm=tm, tiles_per_seq=seq // tm),
        grid=(m // tm,),
        in_specs=[col(0), col(1), col(2), col(3), col(4), halo(1), halo(2),
                  whole((CONV_K, w)), whole((1, w)), whole((1, w)),
                  whole((GMLP_GROUPS, CHUNK, CHUNK)), whole((CHUNK, GMLP_GROUPS))],
        out_specs=pl.BlockSpec((tm, 2 * w), lambda i: (i, 0)),
        out_shape=jax.ShapeDtypeStruct((m, 2 * w), BF16),
        compiler_params=_params("parallel"),
        name="mixer_ab",
    )(proj_ab, proj_ab, proj_ab, proj_ab, proj_ab, proj_ab, proj_ab,
      conv_w, ln_g.reshape(-1, 1, w), ln_b.reshape(-1, 1, w), ws, bs_t)


def _causal_bucket(n):
    max_exact = NUM_BUCKETS // 2
    nf = jnp.maximum(n, 1).astype(F32)
    large = max_exact + (jnp.log(nf / max_exact) / math.log(MAX_DISTANCE / max_exact)
                         * (NUM_BUCKETS - max_exact)).astype(jnp.int32)
    large = jnp.minimum(large, NUM_BUCKETS - 1)
    return jnp.where(n < max_exact, n, large)


def _near_bias_tiles(rel_bias, tq, tk):
    r = tq // tk
    rb = rel_bias.astype(F32)
    rb = jnp.transpose((rb - rb[NUM_BUCKETS - 1]) * LOG2E).reshape(DIFF_HEADS, 2, NUM_BUCKETS)

    def lookup(dist):
        bucket = _causal_bucket(jnp.maximum(dist, 0))
        tiles = jnp.zeros((DIFF_HEADS, 2) + dist.shape, F32)
        for b in range(NUM_BUCKETS - 1):
            tiles = jnp.where(bucket == b, rb[:, :, b].reshape((DIFF_HEADS, 2) + (1,) * dist.ndim),
                              tiles)
        return jnp.where(dist >= 0, tiles, MASK_VALUE)

    qpos = jnp.arange(tq)[None, :]
    kpos = jnp.arange(tk)[:, None]
    diag = lookup(jnp.stack([qpos - kpos - e * tk for e in range(r)], axis=0))
    c = MAX_DISTANCE
    corner = lookup(jnp.arange(c)[None, :] - jnp.arange(c)[:, None] + c)
    return diag, corner


def _diff_attn_kernel(q_ref, k_ref, v_ref, bias_ref, corner_ref, lq_ref, lk_ref, g_ref, o_ref,
                      acc_ref, m_ref, l_ref, t_ref, pm_ref, qt_ref, *, lam_init, tq, tk):
    qi = pl.program_id(2)
    dh = DIFF_HEAD_DIM
    sub = V7X_SUBLANES
    logit_scale = dh ** -0.5 * LOG2E
    acc_ref[...] = jnp.zeros_like(acc_ref)
    m_ref[...] = jnp.full_like(m_ref, MASK_VALUE)
    l_ref[...] = jnp.zeros_like(l_ref)
    for mp in range(2):
        q = q_ref[:, mp * dh:(mp + 1) * dh].astype(F32)
        qt_ref[mp] = jnp.transpose(q).astype(BF16)

    def fold_keys(x, op):
        out = x[0:sub]
        for i in range(1, x.shape[0] // sub):
            out = op(out, x[i * sub:(i + 1) * sub])
        return out

    def scores(j, slot):
        start = pl.multiple_of(j * tk, tk)
        is_below = (j == qi - 1).astype(F32)
        c = MAX_DISTANCE
        for mp in range(2):
            k = k_ref[pl.ds(start, tk), mp * dh:(mp + 1) * dh]
            t = jnp.dot(k, qt_ref[mp], preferred_element_type=F32) * logit_scale
            head, tail = t[0:tk - c], t[tk - c:tk]
            tail = jnp.concatenate([tail[:, 0:c] + is_below * corner_ref[mp], tail[:, c:]], axis=1)
            t_ref[slot, mp, 0:tk - c, :] = head
            t_ref[slot, mp, tk - c:tk, :] = tail
            pm_ref[slot, mp] = jnp.maximum(fold_keys(head, jnp.maximum),
                                           fold_keys(tail, jnp.maximum))

    def accumulate(j, slot, bias_tile):
        start = pl.multiple_of(j * tk, tk)
        v_t = jnp.transpose(v_ref[pl.ds(start, tk), :])
        for mp in range(2):
            t = t_ref[slot, mp]
            if bias_tile is None:
                pm = pm_ref[slot, mp]
            else:
                t = t + bias_ref[mp, bias_tile]
                pm = fold_keys(t, jnp.maximum)
            m_prev = m_ref[mp]
            m_new = jnp.maximum(m_prev, jnp.max(pm, axis=0, keepdims=True))
            p = jnp.exp2(t - m_new)
            a = jnp.exp2(m_prev - m_new)
            l_ref[mp] = a * l_ref[mp] + fold_keys(p, jnp.add)
            m_ref[mp] = m_new
            pv = jnp.dot(v_t, p.astype(BF16), preferred_element_type=F32)
            acc_ref[mp] = a * acc_ref[mp] + pv

    def pipelined(j, count):
        for n in range(count):
            scores(j + n + 1, (n + 1) % 2)
            accumulate(j + n, n % 2, None)

    quads = qi // 4
    rem = qi % 4
    scores(0, 0)

    @pl.loop(0, quads)
    def _(i):
        pipelined(4 * i, 4)

    @pl.when(rem >= 2)
    def _():
        pipelined(4 * quads, 2)

    @pl.when(rem % 2 == 0)
    def _():
        accumulate(qi, 0, 0)

    @pl.when(rem % 2 == 1)
    def _():
        scores(qi, 1)
        accumulate(qi - 1, 0, None)
        accumulate(qi, 1, 0)

    outs = []
    for mp in range(2):
        l = jnp.sum(l_ref[mp], axis=0, keepdims=True)
        outs.append(acc_ref[mp] / l)
    sums = jnp.sum(lq_ref[...] * lk_ref[...], axis=-1, keepdims=True)
    ex = jnp.exp(sums)
    lam = ex[0:1, :] - ex[1:2, :] + lam_init
    o = jnp.transpose(outs[0] - lam * outs[1])
    o = o * lax.rsqrt(jnp.mean(o * o, axis=-1, keepdims=True) + LN_EPS) * g_ref[...]
    o_ref[...] = (o * (1.0 - lam_init)).astype(o_ref.dtype)


def _diff_attention(qkv, bias_tiles, lam_q, lam_k, subln_g, layer, lam_init, bsz, seq, tq, tk):
    m = qkv.shape[0]
    hd = 2 * DIFF_HEAD_DIM
    nq = seq // tq
    assert tq == tk, "the block schedule assumes square score tiles"
    r = tq // tk
    diag, corner = bias_tiles
    c = MAX_DISTANCE
    return pl.pallas_call(
        functools.partial(_diff_attn_kernel, lam_init=lam_init, tq=tq, tk=tk),
        grid=(bsz, DIFF_HEADS, nq),
        in_specs=[
            pl.BlockSpec((tq, hd), lambda b, h, i: (b * nq + i, h)),
            pl.BlockSpec((seq, hd), lambda b, h, i: (b, DIFF_HEADS + h)),
            pl.BlockSpec((seq, hd), lambda b, h, i: (b, 2 * DIFF_HEADS + h)),
            pl.BlockSpec((None, 2, r, tk, tq), lambda b, h, i: (h, 0, 0, 0, 0)),
            pl.BlockSpec((None, 2, c, c), lambda b, h, i: (h, 0, 0, 0)),
            pl.BlockSpec((None, 2, DIFF_HEAD_DIM), lambda b, h, i: (layer, 0, 0)),
            pl.BlockSpec((None, 2, DIFF_HEAD_DIM), lambda b, h, i: (layer, 0, 0)),
            pl.BlockSpec((None, 1, hd), lambda b, h, i: (layer, 0, 0)),
        ],
        out_specs=pl.BlockSpec((tq, hd), lambda b, h, i: (b * nq + i, h)),
        out_shape=jax.ShapeDtypeStruct((m, DIFF_WIDTH), BF16),
        scratch_shapes=[pltpu.VMEM((2, hd, tq), F32),
                        pltpu.VMEM((2, 1, tq), F32),
                        pltpu.VMEM((2, V7X_SUBLANES, tq), F32),
                        pltpu.VMEM((2, 2, tk, tq), F32),
                        pltpu.VMEM((2, 2, V7X_SUBLANES, tq), F32),
                        pltpu.VMEM((2, DIFF_HEAD_DIM, tq), BF16)],
        compiler_params=_params("parallel", "parallel", "arbitrary"),
        name="diff_attention",
    )(qkv, qkv, qkv, diag, corner, lam_q, lam_k, subln_g.reshape(-1, 1, hd))


def _mem_attn_kernel(q_ref, kv_ref, o_ref):
    d = MEM_HEAD_DIM
    scale = d ** -0.5
    for h in range(MEM_HEADS):
        q = q_ref[:, h * d:(h + 1) * d]
        k = kv_ref[:, h * d:(h + 1) * d]
        v = kv_ref[:, D_MODEL + h * d:D_MODEL + (h + 1) * d]
        s = lax.dot_general(q, k, (((1,), (1,)), ((), ())), preferred_element_type=F32) * scale
        s = s - jnp.max(s, axis=-1, keepdims=True)
        p = jnp.exp(s)
        p = p / jnp.sum(p, axis=-1, keepdims=True)
        o_ref[:, h * d:(h + 1) * d] = jnp.dot(p.astype(BF16), v,
                                              preferred_element_type=F32).astype(o_ref.dtype)


def _mem_attention(q, kv, seq, tm):
    m = q.shape[0]
    tiles_per_seq = seq // tm
    return pl.pallas_call(
        _mem_attn_kernel,
        grid=(m // tm,),
        in_specs=[pl.BlockSpec((tm, D_MODEL), lambda i: (i, 0)),
                  pl.BlockSpec((N_MEM, 2 * D_MODEL), lambda i: (i // tiles_per_seq, 0))],
        out_specs=pl.BlockSpec((tm, D_MODEL), lambda i: (i, 0)),
        out_shape=jax.ShapeDtypeStruct((m, D_MODEL), BF16),
        compiler_params=_params("parallel"),
        name="mem_attention",
    )(q, kv)


def _ffn_kernel(xb_ref, wg_ref, wu_ref, wd_ref, res_ref, g_ref, b_ref, o_ref, ob_ref, acc_ref):
    f = pl.program_id(1)

    def slab(first):
        xb = xb_ref[...]
        gate = jnp.dot(xb, wg_ref[...], preferred_element_type=F32)
        up = jnp.dot(xb, wu_ref[...], preferred_element_type=F32)
        hid = (gate * (1.0 / (1.0 + jnp.exp(-gate))) * up).astype(BF16)
        part = jnp.dot(hid, wd_ref[...], preferred_element_type=F32)
        acc_ref[...] = part if first else acc_ref[...] + part

    @pl.when(f == 0)
    def _():
        slab(True)

    @pl.when(f > 0)
    def _():
        slab(False)

    @pl.when(f == pl.num_programs(1) - 1)
    def _():
        y = _layer_norm_rows(ALPHA * res_ref[...] + acc_ref[...], g_ref[...], b_ref[...])
        o_ref[...] = y
        ob_ref[...] = y.astype(BF16)


def _ffn(xb, w_gu, w_down, layer, res, g, b, tm, tf):
    m, d = xb.shape
    nf = D_FF // tf
    return pl.pallas_call(
        _ffn_kernel,
        grid=(m // tm, nf),
        in_specs=[pl.BlockSpec((tm, d), lambda i, f: (i, 0)),
                  pl.BlockSpec((None, d, tf), lambda i, f: (layer, 0, f)),
                  pl.BlockSpec((None, d, tf), lambda i, f: (layer, 0, nf + f)),
                  pl.BlockSpec((None, tf, d), lambda i, f: (layer, f, 0)),
                  pl.BlockSpec((tm, d), lambda i, f: (i, 0)),
                  pl.BlockSpec((None, 1, d), lambda i, f: (layer, 0, 0)),
                  pl.BlockSpec((None, 1, d), lambda i, f: (layer, 0, 0))],
        out_specs=[pl.BlockSpec((tm, d), lambda i, f: (i, 0)),
                   pl.BlockSpec((tm, d), lambda i, f: (i, 0))],
        out_shape=[jax.ShapeDtypeStruct((m, d), F32),
                   jax.ShapeDtypeStruct((m, d), BF16)],
        scratch_shapes=[pltpu.VMEM((tm, d), F32)],
        compiler_params=_params("parallel", "arbitrary"),
        name="ffn",
    )(xb, w_gu, w_gu, w_down, res, g.reshape(-1, 1, d), b.reshape(-1, 1, d))


ATTN_TQ = 512
ATTN_TK = 512


def kernel(x, mem, rel_bias, w_in, conv_w, gmlp_ln_g, gmlp_ln_b, gmlp_ws, gmlp_bs,
           diff_lambda_q, diff_lambda_k, diff_subln_g, w_mix_out, ln_mix_g, ln_mix_b,
           w_mem_q, w_mem_kv, w_mem_out, ln_mem_g, ln_mem_b,
           w_ffn_gu, w_ffn_down, ln_ffn_g, ln_ffn_b):
    bsz, seq, d = x.shape
    m = bsz * seq
    x = x.reshape(m, d)
    xb = x.astype(BF16)
    memb = mem.reshape(bsz * N_MEM, d).astype(BF16)
    bias_tiles = _near_bias_tiles(rel_bias, ATTN_TQ, ATTN_TK)
    w_in, w_mix_out, w_mem_q, w_mem_kv, w_mem_out, w_ffn_gu, w_ffn_down = (
        w.astype(BF16) for w in (w_in, w_mix_out, w_mem_q, w_mem_kv, w_mem_out,
                                 w_ffn_gu, w_ffn_down))
    gmlp_bs_t = jnp.transpose(gmlp_bs, (0, 2, 1))
    for l in range(DEPTH):
        lam_init = 0.8 - 0.6 * math.exp(-0.3 * l)
        proj_ab = _matmul(xb, w_in, l, 0, AB_COLS, F32, tm=2048, tn=512)
        qkv = _matmul(xb, w_in, l, AB_COLS, QKV_COLS, BF16, tm=2048, tn=512)
        y_ab = _mixer_ab(proj_ab, conv_w, gmlp_ln_g, gmlp_ln_b, gmlp_ws, gmlp_bs_t, l, seq, tm=256)
        y_c = _diff_attention(qkv, bias_tiles, diff_lambda_q, diff_lambda_k, diff_subln_g, l,
                              lam_init, bsz, seq, ATTN_TQ, ATTN_TK)
        x, xb = _matmul_res_ln([y_ab, y_c], w_mix_out, l, x, ln_mix_g, ln_mix_b, tm=512)

        q_mem = _matmul(xb, w_mem_q, l, 0, D_MODEL, BF16, tm=2048, tn=1024)
        kv_mem = _matmul(memb, w_mem_kv, l, 0, 2 * D_MODEL, BF16, tm=bsz * N_MEM, tn=1024)
        o_mem = _mem_attention(q_mem, kv_mem, seq, tm=512)
        x, xb = _matmul_res_ln([o_mem], w_mem_out, l, x, ln_mem_g, ln_mem_b, tm=512)

        x, xb = _ffn(xb, w_ffn_gu, w_ffn_down, l, x, ln_ffn_g, ln_ffn_b, tm=512, tf=512)
    return x.reshape(bsz, seq, d)
```

```python
import functools
import math

import jax
import jax.numpy as jnp
from jax import lax
from jax.experimental import pallas as pl
from jax.experimental.pallas import tpu as pltpu

D_MODEL = 2048
DEPTH = 2
N_MEM = 256
MEM_HEADS = 4
MEM_HEAD_DIM = D_MODEL // MEM_HEADS
CONV_WIDTH = D_MODEL // 4
CONV_K = 3
GMLP_WIDTH = D_MODEL // 4
GMLP_GROUPS = 4
GMLP_GROUP_DIM = GMLP_WIDTH // GMLP_GROUPS
CHUNK = 128
DIFF_HEAD_DIM = 128
DIFF_WIDTH = D_MODEL // 2
DIFF_HEADS = DIFF_WIDTH // (2 * DIFF_HEAD_DIM)
AB_COLS = 3 * CONV_WIDTH + 2 * GMLP_WIDTH
QKV_COLS = 3 * DIFF_WIDTH
NUM_BUCKETS = 32
MAX_DISTANCE = 128
D_FF = -(-8 * D_MODEL // (3 * 256)) * 256
ALPHA = (2 * DEPTH) ** 0.25
LN_EPS = 1e-5

V7X_VMEM_BYTES = 64 * 1024 * 1024
V7X_LANES = 128
V7X_SUBLANES = 8
VMEM_LIMIT_BYTES = V7X_VMEM_BYTES - 8 * 1024 * 1024

MASK_VALUE = -1e30
LOG2E = math.log2(math.e)
ATTN_LOGIT_SCALE = DIFF_HEAD_DIM ** -0.5 * LOG2E

F32 = jnp.float32
BF16 = jnp.bfloat16


def _params(*semantics):
    return pltpu.CompilerParams(dimension_semantics=semantics,
                                vmem_limit_bytes=VMEM_LIMIT_BYTES)


def _layer_norm_rows(x, g, b):
    mu = jnp.mean(x, axis=-1, keepdims=True)
    xc = x - mu
    var = jnp.mean(xc * xc, axis=-1, keepdims=True)
    return xc * lax.rsqrt(var + LN_EPS) * g + b


def _mm_kernel(a_ref, w_ref, o_ref, *, scaled_tiles, scale):
    acc = jnp.dot(a_ref[...], w_ref[...], preferred_element_type=F32)
    if scaled_tiles:
        acc = acc * jnp.where(pl.program_id(1) < scaled_tiles, scale, 1.0)
    o_ref[...] = acc.astype(o_ref.dtype)


def _matmul(a, w, layer, col0, n, out_dtype, tm, tn, scaled_cols=0, scale=1.0):
    m, k = a.shape
    j0 = col0 // tn
    return pl.pallas_call(
        functools.partial(_mm_kernel, scaled_tiles=scaled_cols // tn, scale=scale),
        grid=(m // tm, n // tn),
        in_specs=[pl.BlockSpec((tm, k), lambda i, j: (i, 0)),
                  pl.BlockSpec((None, k, tn), lambda i, j: (layer, 0, j0 + j))],
        out_specs=pl.BlockSpec((tm, tn), lambda i, j: (i, j)),
        out_shape=jax.ShapeDtypeStruct((m, n), out_dtype),
        compiler_params=_params("parallel", "parallel"),
        name="matmul",
    )(a, w)


def _mm_res_ln_kernel(*refs, n_a):
    a_refs = refs[:n_a]
    w_ref, res_ref, g_ref, b_ref, o_ref, ob_ref = refs[n_a:]
    acc = None
    row = 0
    for a_ref in a_refs:
        k = a_ref.shape[1]
        part = jnp.dot(a_ref[...], w_ref[row:row + k, :], preferred_element_type=F32)
        acc = part if acc is None else acc + part
        row += k
    y = _layer_norm_rows(ALPHA * res_ref[...] + acc, g_ref[...], b_ref[...])
    o_ref[...] = y
    ob_ref[...] = y.astype(BF16)


def _matmul_res_ln(a_list, w, layer, res, g, b, tm):
    m = res.shape[0]
    _, k, n = w.shape
    in_specs = [pl.BlockSpec((tm, a.shape[1]), lambda i: (i, 0)) for a in a_list]
    in_specs += [pl.BlockSpec((None, k, n), lambda i: (layer, 0, 0)),
                 pl.BlockSpec((tm, n), lambda i: (i, 0)),
                 pl.BlockSpec((None, 1, n), lambda i: (layer, 0, 0)),
                 pl.BlockSpec((None, 1, n), lambda i: (layer, 0, 0))]
    return pl.pallas_call(
        functools.partial(_mm_res_ln_kernel, n_a=len(a_list)),
        grid=(m // tm,),
        in_specs=in_specs,
        out_specs=[pl.BlockSpec((tm, n), lambda i: (i, 0)),
                   pl.BlockSpec((tm, n), lambda i: (i, 0))],
        out_shape=[jax.ShapeDtypeStruct((m, n), F32),
                   jax.ShapeDtypeStruct((m, n), BF16)],
        compiler_params=_params("parallel"),
        name="matmul_res_ln",
    )(*a_list, w, res, g.reshape(-1, 1, n), b.reshape(-1, 1, n))


def _mixer_ab_kernel(bg_ref, cg_ref, h_ref, u_ref, v_ref, cgp_ref, hp_ref,
                     cw_ref, lg_ref, lb_ref, ws_ref, bst_ref, o_ref, *, tm, tiles_per_seq):
    i = pl.program_id(0)
    z = cg_ref[...] * h_ref[...]
    zp = cgp_ref[...] * hp_ref[...]
    zp = jnp.where(i % tiles_per_seq == 0, 0.0, zp)
    rows = lax.broadcasted_iota(jnp.int32, (tm, 1), 0)
    last = zp[V7X_SUBLANES - 1:V7X_SUBLANES, :]
    last2 = zp[V7X_SUBLANES - 2:V7X_SUBLANES - 1, :]
    z1 = jnp.where(rows == 0, last, pltpu.roll(z, 1, axis=0))
    z2 = jnp.where(rows == 0, last2, jnp.where(rows == 1, last, pltpu.roll(z, 2, axis=0)))
    cw = cw_ref[...]
    conv = cw[0:1, :] * z2 + cw[1:2, :] * z1 + cw[2:3, :] * z
    o_ref[:, 0:CONV_WIDTH] = (bg_ref[...] * conv).astype(o_ref.dtype)

    u = jax.nn.gelu(u_ref[...])
    vn = _layer_norm_rows(jax.nn.gelu(v_ref[...]), lg_ref[...], lb_ref[...]).astype(BF16)
    tr = lax.broadcasted_iota(jnp.int32, (CHUNK, CHUNK), 0)
    tc = lax.broadcasted_iota(jnp.int32, (CHUNK, CHUNK), 1)
    causal = tr >= tc
    bst = bst_ref[...]
    for g in range(GMLP_GROUPS):
        wg = jnp.where(causal, ws_ref[g], 0.0).astype(BF16)
        bias = bst[:, g:g + 1]
        c0 = g * GMLP_GROUP_DIM
        for c in range(tm // CHUNK):
            r0 = c * CHUNK
            sv = jnp.dot(wg, vn[r0:r0 + CHUNK, c0:c0 + GMLP_GROUP_DIM],
                         preferred_element_type=F32) + bias
            o_ref[r0:r0 + CHUNK, CONV_WIDTH + c0:CONV_WIDTH + c0 + GMLP_GROUP_DIM] = (
                u[r0:r0 + CHUNK, c0:c0 + GMLP_GROUP_DIM] * sv).astype(o_ref.dtype)


def _mixer_ab(proj_ab, conv_w, ln_g, ln_b, ws, bs_t, layer, seq, tm):
    m = proj_ab.shape[0]
    w = CONV_WIDTH
    halo_blocks = tm // V7X_SUBLANES

    def col(c):
        return pl.BlockSpec((tm, w), lambda i, c=c: (i, c))

    def halo(c):
        return pl.BlockSpec((V7X_SUBLANES, w),
                            lambda i, c=c: (jnp.maximum(i * halo_blocks - 1, 0), c))

    def whole(shape):
        return pl.BlockSpec((None,) + shape, lambda i: (layer,) + (0,) * len(shape))

    return pl.pallas_call(
        functools.partial(_mixer_ab_kernel, tm=tm, tiles_per_seq=seq // tm),
        grid=(m // tm,),
        in_specs=[col(0), col(1), col(2), col(3), col(4), halo(1), halo(2),
                  whole((CONV_K, w)), whole((1, w)), whole((1, w)),
                  whole((GMLP_GROUPS, CHUNK, CHUNK)), whole((CHUNK, GMLP_GROUPS))],
        out_specs=pl.BlockSpec((tm, 2 * w), lambda i: (i, 0)),
        out_shape=jax.ShapeDtypeStruct((m, 2 * w), BF16),
        compiler_params=_params("parallel"),
        name="mixer_ab",
    )(proj_ab, proj_ab, proj_ab, proj_ab, proj_ab, proj_ab, proj_ab,
      conv_w, ln_g.reshape(-1, 1, w), ln_b.reshape(-1, 1, w), ws, bs_t)


def _causal_bucket(n):
    max_exact = NUM_BUCKETS // 2
    nf = jnp.maximum(n, 1).astype(F32)
    large = max_exact + (jnp.log(nf / max_exact) / math.log(MAX_DISTANCE / max_exact)
                         * (NUM_BUCKETS - max_exact)).astype(jnp.int32)
    large = jnp.minimum(large, NUM_BUCKETS - 1)
    return jnp.where(n < max_exact, n, large)


def _near_bias_tiles(rel_bias, tq, tk):
    r = tq // tk
    rb = rel_bias.astype(F32)
    rb = jnp.transpose((rb - rb[NUM_BUCKETS - 1]) * LOG2E).reshape(DIFF_HEADS, 2, NUM_BUCKETS)

    def lookup(dist):
        bucket = _causal_bucket(jnp.maximum(dist, 0))
        tiles = jnp.zeros((DIFF_HEADS, 2) + dist.shape, F32)
        for b in range(NUM_BUCKETS - 1):
            tiles = jnp.where(bucket == b, rb[:, :, b].reshape((DIFF_HEADS, 2) + (1,) * dist.ndim),
                              tiles)
        return jnp.where(dist >= 0, tiles, MASK_VALUE)

    qpos = jnp.arange(tq)[None, :]
    kpos = jnp.arange(tk)[:, None]
    diag = lookup(jnp.stack([qpos - kpos - e * tk for e in range(r)], axis=0))
    c = MAX_DISTANCE
    corner = lookup(jnp.arange(c)[None, :] - jnp.arange(c)[:, None] + c)
    return diag, corner


def _diff_attn_kernel(q_ref, k_ref, v_ref, bias_ref, corner_ref, lq_ref, lk_ref, g_ref, o_ref,
                      acc_ref, m_ref, l_ref, t_ref, pm_ref, qt_ref, *, lam_init, tq, tk):
    qi = pl.program_id(2)
    dh = DIFF_HEAD_DIM
    sub = V7X_SUBLANES
    acc_ref[...] = jnp.zeros_like(acc_ref)
    m_ref[...] = jnp.full_like(m_ref, MASK_VALUE)
    l_ref[...] = jnp.zeros_like(l_ref)
    for mp in range(2):
        q = q_ref[:, mp * dh:(mp + 1) * dh].astype(F32)
        qt_ref[mp] = jnp.transpose(q).astype(BF16)

    def fold_keys(x, op):
        out = x[0:sub]
        for i in range(1, x.shape[0] // sub):
            out = op(out, x[i * sub:(i + 1) * sub])
        return out

    def scores(j, slot):
        start = pl.multiple_of(j * tk, tk)
        is_below = (j == qi - 1).astype(F32)
        c = MAX_DISTANCE
        for mp in range(2):
            k = k_ref[pl.ds(start, tk), mp * dh:(mp + 1) * dh]
            t = jnp.dot(k, qt_ref[mp], preferred_element_type=F32)
            head, tail = t[0:tk - c], t[tk - c:tk]
            tail = jnp.concatenate([tail[:, 0:c] + is_below * corner_ref[mp], tail[:, c:]], axis=1)
            t_ref[slot, mp, 0:tk - c, :] = head
            t_ref[slot, mp, tk - c:tk, :] = tail
            pm_ref[slot, mp] = jnp.maximum(fold_keys(head, jnp.maximum),
                                           fold_keys(tail, jnp.maximum))

    def accumulate(j, slot, bias_tile):
        start = pl.multiple_of(j * tk, tk)
        v_t = jnp.transpose(v_ref[pl.ds(start, tk), :])
        for mp in range(2):
            t = t_ref[slot, mp]
            if bias_tile is None:
                pm = pm_ref[slot, mp]
            else:
                t = t + bias_ref[mp, bias_tile]
                pm = fold_keys(t, jnp.maximum)
            m_prev = m_ref[mp]
            m_new = jnp.maximum(m_prev, jnp.max(pm, axis=0, keepdims=True))
            p = jnp.exp2(t - m_new)
            a = jnp.exp2(m_prev - m_new)
            l_ref[mp] = a * l_ref[mp] + fold_keys(p, jnp.add)
            m_ref[mp] = m_new
            pv = jnp.dot(v_t, p.astype(BF16), preferred_element_type=F32)
            acc_ref[mp] = a * acc_ref[mp] + pv

    def pipelined(j, count):
        for n in range(count):
            scores(j + n + 1, (n + 1) % 2)
            accumulate(j + n, n % 2, None)

    quads = qi // 4
    rem = qi % 4
    scores(0, 0)

    @pl.loop(0, quads)
    def _(i):
        pipelined(4 * i, 4)

    @pl.when(rem >= 2)
    def _():
        pipelined(4 * quads, 2)

    @pl.when(rem % 2 == 0)
    def _():
        accumulate(qi, 0, 0)

    @pl.when(rem % 2 == 1)
    def _():
        scores(qi, 1)
        accumulate(qi - 1, 0, None)
        accumulate(qi, 1, 0)

    outs = []
    for mp in range(2):
        l = jnp.sum(l_ref[mp], axis=0, keepdims=True)
        outs.append(acc_ref[mp] / l)
    sums = jnp.sum(lq_ref[...] * lk_ref[...], axis=-1, keepdims=True)
    ex = jnp.exp(sums)
    lam = ex[0:1, :] - ex[1:2, :] + lam_init
    o = jnp.transpose(outs[0] - lam * outs[1])
    o = o * lax.rsqrt(jnp.mean(o * o, axis=-1, keepdims=True) + LN_EPS) * g_ref[...]
    o_ref[...] = (o * (1.0 - lam_init)).astype(o_ref.dtype)


def _diff_attention(qkv, bias_tiles, lam_q, lam_k, subln_g, layer, lam_init, bsz, seq, tq, tk):
    m = qkv.shape[0]
    hd = 2 * DIFF_HEAD_DIM
    nq = seq // tq
    assert tq == tk, "the block schedule assumes square score tiles"
    r = tq // tk
    diag, corner = bias_tiles
    c = MAX_DISTANCE
    return pl.pallas_call(
        functools.partial(_diff_attn_kernel, lam_init=lam_init, tq=tq, tk=tk),
        grid=(bsz, DIFF_HEADS, nq),
        in_specs=[
            pl.BlockSpec((tq, hd), lambda b, h, i: (b * nq + i, h)),
            pl.BlockSpec((seq, hd), lambda b, h, i: (b, DIFF_HEADS + h)),
            pl.BlockSpec((seq, hd), lambda b, h, i: (b, 2 * DIFF_HEADS + h)),
            pl.BlockSpec((None, 2, r, tk, tq), lambda b, h, i: (h, 0, 0, 0, 0)),
            pl.BlockSpec((None, 2, c, c), lambda b, h, i: (h, 0, 0, 0)),
            pl.BlockSpec((None, 2, DIFF_HEAD_DIM), lambda b, h, i: (layer, 0, 0)),
            pl.BlockSpec((None, 2, DIFF_HEAD_DIM), lambda b, h, i: (layer, 0, 0)),
            pl.BlockSpec((None, 1, hd), lambda b, h, i: (layer, 0, 0)),
        ],
        out_specs=pl.BlockSpec((tq, hd), lambda b, h, i: (b * nq + i, h)),
        out_shape=jax.ShapeDtypeStruct((m, DIFF_WIDTH), BF16),
        scratch_shapes=[pltpu.VMEM((2, hd, tq), F32),
                        pltpu.VMEM((2, 1, tq), F32),
                        pltpu.VMEM((2, V7X_SUBLANES, tq), F32),
                        pltpu.VMEM((2, 2, tk, tq), F32),
                        pltpu.VMEM((2, 2, V7X_SUBLANES, tq), F32),
                        pltpu.VMEM((2, DIFF_HEAD_DIM, tq), BF16)],
        compiler_params=_params("parallel", "parallel", "arbitrary"),
        name="diff_attention",
    )(qkv, qkv, qkv, diag, corner, lam_q, lam_k, subln_g.reshape(-1, 1, hd))


def _mem_attn_kernel(q_ref, kv_ref, o_ref):
    d = MEM_HEAD_DIM
    scale = d ** -0.5
    for h in range(MEM_HEADS):
        q = q_ref[:, h * d:(h + 1) * d]
        k = kv_ref[:, h * d:(h + 1) * d]
        v = kv_ref[:, D_MODEL + h * d:D_MODEL + (h + 1) * d]
        s = lax.dot_general(q, k, (((1,), (1,)), ((), ())), preferred_element_type=F32) * scale
        s = s - jnp.max(s, axis=-1, keepdims=True)
        p = jnp.exp(s)
        p = p / jnp.sum(p, axis=-1, keepdims=True)
        o_ref[:, h * d:(h + 1) * d] = jnp.dot(p.astype(BF16), v,
                                              preferred_element_type=F32).astype(o_ref.dtype)


def _mem_attention(q, kv, seq, tm):
    m = q.shape[0]
    tiles_per_seq = seq // tm
    return pl.pallas_call(
        _mem_attn_kernel,
        grid=(m // tm,),
        in_specs=[pl.BlockSpec((tm, D_MODEL), lambda i: (i, 0)),
                  pl.BlockSpec((N_MEM, 2 * D_MODEL), lambda i: (i // tiles_per_seq, 0))],
        out_specs=pl.BlockSpec((tm, D_MODEL), lambda i: (i, 0)),
        out_shape=jax.ShapeDtypeStruct((m, D_MODEL), BF16),
        compiler_params=_params("parallel"),
        name="mem_attention",
    )(q, kv)


def _ffn_kernel(xb_ref, wg_ref, wu_ref, wd_ref, res_ref, g_ref, b_ref, o_ref, ob_ref, acc_ref):
    f = pl.program_id(1)

    def slab(first):
        xb = xb_ref[...]
        gate = jnp.dot(xb, wg_ref[...], preferred_element_type=F32)
        up = jnp.dot(xb, wu_ref[...], preferred_element_type=F32)
        hid = (gate * (1.0 / (1.0 + jnp.exp(-gate))) * up).astype(BF16)
        part = jnp.dot(hid, wd_ref[...], preferred_element_type=F32)
        acc_ref[...] = part if first else acc_ref[...] + part

    @pl.when(f == 0)
    def _():
        slab(True)

    @pl.when(f > 0)
    def _():
        slab(False)

    @pl.when(f == pl.num_programs(1) - 1)
    def _():
        y = _layer_norm_rows(ALPHA * res_ref[...] + acc_ref[...], g_ref[...], b_ref[...])
        o_ref[...] = y
        ob_ref[...] = y.astype(BF16)


def _ffn(xb, w_gu, w_down, layer, res, g, b, tm, tf):
    m, d = xb.shape
    nf = D_FF // tf
    return pl.pallas_call(
        _ffn_kernel,
        grid=(m // tm, nf),
        in_specs=[pl.BlockSpec((tm, d), lambda i, f: (i, 0)),
                  pl.BlockSpec((None, d, tf), lambda i, f: (layer, 0, f)),
                  pl.BlockSpec((None, d, tf), lambda i, f: (layer, 0, nf + f)),
                  pl.BlockSpec((None, tf, d), lambda i, f: (layer, f, 0)),
                  pl.BlockSpec((tm, d), lambda i, f: (i, 0)),
                  pl.BlockSpec((None, 1, d), lambda i, f: (layer, 0, 0)),
                  pl.BlockSpec((None, 1, d), lambda i, f: (layer, 0, 0))],
        out_specs=[pl.BlockSpec((tm, d), lambda i, f: (i, 0)),
                   pl.BlockSpec((tm, d), lambda i, f: (i, 0))],
        out_shape=[jax.ShapeDtypeStruct((m, d), F32),
                   jax.ShapeDtypeStruct((m, d), BF16)],
        scratch_shapes=[pltpu.VMEM((tm, d), F32)],
        compiler_params=_params("parallel", "arbitrary"),
        name="ffn",
    )(xb, w_gu, w_gu, w_down, res, g.reshape(-1, 1, d), b.reshape(-1, 1, d))


ATTN_TQ = 512
ATTN_TK = 512


def kernel(x, mem, rel_bias, w_in, conv_w, gmlp_ln_g, gmlp_ln_b, gmlp_ws, gmlp_bs,
           diff_lambda_q, diff_lambda_k, diff_subln_g, w_mix_out, ln_mix_g, ln_mix_b,
           w_mem_q, w_mem_kv, w_mem_out, ln_mem_g, ln_mem_b,
           w_ffn_gu, w_ffn_down, ln_ffn_g, ln_ffn_b):
    bsz, seq, d = x.shape
    m = bsz * seq
    x = x.reshape(m, d)
    xb = x.astype(BF16)
    memb = mem.reshape(bsz * N_MEM, d).astype(BF16)
    bias_tiles = _near_bias_tiles(rel_bias, ATTN_TQ, ATTN_TK)
    w_in, w_mix_out, w_mem_q, w_mem_kv, w_mem_out, w_ffn_gu, w_ffn_down = (
        w.astype(BF16) for w in (w_in, w_mix_out, w_mem_q, w_mem_kv, w_mem_out,
                                 w_ffn_gu, w_ffn_down))
    gmlp_bs_t = jnp.transpose(gmlp_bs, (0, 2, 1))
    for l in range(DEPTH):
        lam_init = 0.8 - 0.6 * math.exp(-0.3 * l)
        proj_ab = _matmul(xb, w_in, l, 0, AB_COLS, F32, tm=2048, tn=512)
        qkv = _matmul(xb, w_in, l, AB_COLS, QKV_COLS, BF16, tm=2048, tn=512,
                      scaled_cols=DIFF_WIDTH, scale=ATTN_LOGIT_SCALE)
        y_ab = _mixer_ab(proj_ab, conv_w, gmlp_ln_g, gmlp_ln_b, gmlp_ws, gmlp_bs_t, l, seq, tm=256)
        y_c = _diff_attention(qkv, bias_tiles, diff_lambda_q, diff_lambda_k, diff_subln_g, l,
                              lam_init, bsz, seq, ATTN_TQ, ATTN_TK)
        x, xb = _matmul_res_ln([y_ab, y_c], w_mix_out, l, x, ln_mix_g, ln_mix_b, tm=512)

        q_mem = _matmul(xb, w_mem_q, l, 0, D_MODEL, BF16, tm=2048, tn=1024)
        kv_mem = _matmul(memb, w_mem_kv, l, 0, 2 * D_MODEL, BF16, tm=bsz * N_MEM, tn=1024)
        o_mem = _mem_attention(q_mem, kv_mem, seq, tm=512)
        x, xb = _matmul_res_ln([o_mem], w_mem_out, l, x, ln_mem_g, ln_mem_b, tm=512)

        x, xb = _ffn(xb, w_ffn_gu, w_ffn_down, l, x, ln_ffn_g, ln_ffn_b, tm=512, tf=512)
    return x.reshape(bsz, seq, d)
```

```python
import functools
import math

import jax
import jax.numpy as jnp
from jax import lax
from jax.experimental import pallas as pl
from jax.experimental.pallas import tpu as pltpu

D_MODEL = 2048
DEPTH = 2
N_MEM = 256
MEM_HEADS = 4
MEM_HEAD_DIM = D_MODEL // MEM_HEADS
CONV_WIDTH = D_MODEL // 4
CONV_K = 3
GMLP_WIDTH = D_MODEL // 4
GMLP_GROUPS = 4
GMLP_GROUP_DIM = GMLP_WIDTH // GMLP_GROUPS
CHUNK = 128
DIFF_HEAD_DIM = 128
DIFF_WIDTH = D_MODEL // 2
DIFF_HEADS = DIFF_WIDTH // (2 * DIFF_HEAD_DIM)
AB_COLS = 3 * CONV_WIDTH + 2 * GMLP_WIDTH
QKV_COLS = 3 * DIFF_WIDTH
NUM_BUCKETS = 32
MAX_DISTANCE = 128
D_FF = -(-8 * D_MODEL // (3 * 256)) * 256
ALPHA = (2 * DEPTH) ** 0.25
LN_EPS = 1e-5

V7X_VMEM_BYTES = 64 * 1024 * 1024
V7X_LANES = 128
V7X_SUBLANES = 8
VMEM_LIMIT_BYTES = V7X_VMEM_BYTES - 8 * 1024 * 1024

MASK_VALUE = -1e30
LOG2E = math.log2(math.e)
ATTN_LOGIT_SCALE = DIFF_HEAD_DIM ** -0.5 * LOG2E

F32 = jnp.float32
BF16 = jnp.bfloat16


def _params(*semantics):
    return pltpu.CompilerParams(dimension_semantics=semantics,
                                vmem_limit_bytes=VMEM_LIMIT_BYTES)


def _layer_norm_rows(x, g, b):
    mu = jnp.mean(x, axis=-1, keepdims=True)
    xc = x - mu
    var = jnp.mean(xc * xc, axis=-1, keepdims=True)
    return xc * lax.rsqrt(var + LN_EPS) * g + b


def _mm_kernel(a_ref, w_ref, o_ref, *, scaled_tiles, scale):
    acc = jnp.dot(a_ref[...], w_ref[...], preferred_element_type=F32)
    if scaled_tiles:
        acc = acc * jnp.where(pl.program_id(1) < scaled_tiles, scale, 1.0)
    o_ref[...] = acc.astype(o_ref.dtype)


def _matmul(a, w, layer, col0, n, out_dtype, tm, tn, scaled_cols=0, scale=1.0):
    m, k = a.shape
    j0 = col0 // tn
    return pl.pallas_call(
        functools.partial(_mm_kernel, scaled_tiles=scaled_cols // tn, scale=scale),
        grid=(m // tm, n // tn),
        in_specs=[pl.BlockSpec((tm, k), lambda i, j: (i, 0)),
                  pl.BlockSpec((None, k, tn), lambda i, j: (layer, 0, j0 + j))],
        out_specs=pl.BlockSpec((tm, tn), lambda i, j: (i, j)),
        out_shape=jax.ShapeDtypeStruct((m, n), out_dtype),
        compiler_params=_params("parallel", "parallel"),
        name="matmul",
    )(a, w)


def _mm_res_ln_kernel(*refs, n_a):
    a_refs = refs[:n_a]
    w_ref, res_ref, g_ref, b_ref, o_ref, ob_ref = refs[n_a:]
    acc = None
    row = 0
    for a_ref in a_refs:
        k = a_ref.shape[1]
        part = jnp.dot(a_ref[...], w_ref[row:row + k, :], preferred_element_type=F32)
        acc = part if acc is None else acc + part
        row += k
    y = _layer_norm_rows(ALPHA * res_ref[...] + acc, g_ref[...], b_ref[...])
    o_ref[...] = y
    ob_ref[...] = y.astype(BF16)


def _matmul_res_ln(a_list, w, layer, res, g, b, tm):
    m = res.shape[0]
    _, k, n = w.shape
    in_specs = [pl.BlockSpec((tm, a.shape[1]), lambda i: (i, 0)) for a in a_list]
    in_specs += [pl.BlockSpec((None, k, n), lambda i: (layer, 0, 0)),
                 pl.BlockSpec((tm, n), lambda i: (i, 0)),
                 pl.BlockSpec((None, 1, n), lambda i: (layer, 0, 0)),
                 pl.BlockSpec((None, 1, n), lambda i: (layer, 0, 0))]
    return pl.pallas_call(
        functools.partial(_mm_res_ln_kernel, n_a=len(a_list)),
        grid=(m // tm,),
        in_specs=in_specs,
        out_specs=[pl.BlockSpec((tm, n), lambda i: (i, 0)),
                   pl.BlockSpec((tm, n), lambda i: (i, 0))],
        out_shape=[jax.ShapeDtypeStruct((m, n), F32),
                   jax.ShapeDtypeStruct((m, n), BF16)],
        compiler_params=_params("parallel"),
        name="matmul_res_ln",
    )(*a_list, w, res, g.reshape(-1, 1, n), b.reshape(-1, 1, n))


def _mixer_ab_kernel(bg_ref, cg_ref, h_ref, u_ref, v_ref, cgp_ref, hp_ref,
                     cw_ref, lg_ref, lb_ref, ws_ref, bst_ref, o_ref, *, tm, tiles_per_seq):
    i = pl.program_id(0)
    z = cg_ref[...] * h_ref[...]
    zp = cgp_ref[...] * hp_ref[...]
    zp = jnp.where(i % tiles_per_seq == 0, 0.0, zp)
    rows = lax.broadcasted_iota(jnp.int32, (tm, 1), 0)
    last = zp[V7X_SUBLANES - 1:V7X_SUBLANES, :]
    last2 = zp[V7X_SUBLANES - 2:V7X_SUBLANES - 1, :]
    z1 = jnp.where(rows == 0, last, pltpu.roll(z, 1, axis=0))
    z2 = jnp.where(rows == 0, last2, jnp.where(rows == 1, last, pltpu.roll(z, 2, axis=0)))
    cw = cw_ref[...]
    conv = cw[0:1, :] * z2 + cw[1:2, :] * z1 + cw[2:3, :] * z
    o_ref[:, 0:CONV_WIDTH] = (bg_ref[...] * conv).astype(o_ref.dtype)

    u = jax.nn.gelu(u_ref[...])
    vn = _layer_norm_rows(jax.nn.gelu(v_ref[...]), lg_ref[...], lb_ref[...]).astype(BF16)
    tr = lax.broadcasted_iota(jnp.int32, (CHUNK, CHUNK), 0)
    tc = lax.broadcasted_iota(jnp.int32, (CHUNK, CHUNK), 1)
    causal = tr >= tc
    bst = bst_ref[...]
    for g in range(GMLP_GROUPS):
        wg = jnp.where(causal, ws_ref[g], 0.0).astype(BF16)
        bias = bst[:, g:g + 1]
        c0 = g * GMLP_GROUP_DIM
        for c in range(tm // CHUNK):
            r0 = c * CHUNK
            sv = jnp.dot(wg, vn[r0:r0 + CHUNK, c0:c0 + GMLP_GROUP_DIM],
                         preferred_element_type=F32) + bias
            o_ref[r0:r0 + CHUNK, CONV_WIDTH + c0:CONV_WIDTH + c0 + GMLP_GROUP_DIM] = (
                u[r0:r0 + CHUNK, c0:c0 + GMLP_GROUP_DIM] * sv).astype(o_ref.dtype)


def _mixer_ab(proj_ab, conv_w, ln_g, ln_b, ws, bs_t, layer, seq, tm):
    m = proj_ab.shape[0]
    w = CONV_WIDTH
    halo_blocks = tm // V7X_SUBLANES

    def col(c):
        return pl.BlockSpec((tm, w), lambda i, c=c: (i, c))

    def halo(c):
        return pl.BlockSpec((V7X_SUBLANES, w),
                            lambda i, c=c: (jnp.maximum(i * halo_blocks - 1, 0), c))

    def whole(shape):
        return pl.BlockSpec((None,) + shape, lambda i: (layer,) + (0,) * len(shape))

    return pl.pallas_call(
        functools.partial(_mixer_ab_kernel, tm=tm, tiles_per_seq=seq // tm),
        grid=(m // tm,),
        in_specs=[col(0), col(1), col(2), col(3), col(4), halo(1), halo(2),
                  whole((CONV_K, w)), whole((1, w)), whole((1, w)),
                  whole((GMLP_GROUPS, CHUNK, CHUNK)), whole((CHUNK, GMLP_GROUPS))],
        out_specs=pl.BlockSpec((tm, 2 * w), lambda i: (i, 0)),
        out_shape=jax.ShapeDtypeStruct((m, 2 * w), BF16),
        compiler_params=_params("parallel"),
        name="mixer_ab",
    )(proj_ab, proj_ab, proj_ab, proj_ab, proj_ab, proj_ab, proj_ab,
      conv_w, ln_g.reshape(-1, 1, w), ln_b.reshape(-1, 1, w), ws, bs_t)


def _causal_bucket(n):
    max_exact = NUM_BUCKETS // 2
    nf = jnp.maximum(n, 1).astype(F32)
    large = max_exact + (jnp.log(nf / max_exact) / math.log(MAX_DISTANCE / max_exact)
                         * (NUM_BUCKETS - max_exact)).astype(jnp.int32)
    large = jnp.minimum(large, NUM_BUCKETS - 1)
    return jnp.where(n < max_exact, n, large)


def _near_bias_tiles(rel_bias, tq, tk):
    r = tq // tk
    rb = rel_bias.astype(F32)
    rb = jnp.transpose((rb - rb[NUM_BUCKETS - 1]) * LOG2E).reshape(DIFF_HEADS, 2, NUM_BUCKETS)

    def lookup(dist):
        bucket = _causal_bucket(jnp.maximum(dist, 0))
        tiles = jnp.zeros((DIFF_HEADS, 2) + dist.shape, F32)
        for b in range(NUM_BUCKETS - 1):
            tiles = jnp.where(bucket == b, rb[:, :, b].reshape((DIFF_HEADS, 2) + (1,) * dist.ndim),
                              tiles)
        return jnp.where(dist >= 0, tiles, MASK_VALUE)

    qpos = jnp.arange(tq)[None, :]
    kpos = jnp.arange(tk)[:, None]
    diag = lookup(jnp.stack([qpos - kpos - e * tk for e in range(r)], axis=0))
    c = MAX_DISTANCE
    corner = lookup(jnp.arange(c)[None, :] - jnp.arange(c)[:, None] + c)
    return diag, corner


def _diff_attn_kernel(q_ref, qn_ref, k_ref, v_ref, bias_ref, corner_ref, lq_ref, lk_ref, g_ref, o_ref,
                      acc_ref, m_ref, l_ref, t_ref, pm_ref, qt_ref, *, lam_init, tq, tk):
    qi = pl.program_id(2)
    dh = DIFF_HEAD_DIM
    sub = V7X_SUBLANES
    acc_ref[...] = jnp.zeros_like(acc_ref)
    m_ref[...] = jnp.full_like(m_ref, MASK_VALUE)
    l_ref[...] = jnp.zeros_like(l_ref)

    def fold_keys(x, op):
        out = x[0:sub]
        for i in range(1, x.shape[0] // sub):
            out = op(out, x[i * sub:(i + 1) * sub])
        return out

    def scores(j, slot, q_index=qi):
        start = pl.multiple_of(j * tk, tk)
        is_below = (j == q_index - 1).astype(F32)
        c = MAX_DISTANCE
        for mp in range(2):
            k = k_ref[pl.ds(start, tk), mp * dh:(mp + 1) * dh]
            t = jnp.dot(k, qt_ref[mp], preferred_element_type=F32)
            head, tail = t[0:tk - c], t[tk - c:tk]
            tail = jnp.concatenate([tail[:, 0:c] + is_below * corner_ref[mp], tail[:, c:]], axis=1)
            t_ref[slot, mp, 0:tk - c, :] = head
            t_ref[slot, mp, tk - c:tk, :] = tail
            pm_ref[slot, mp] = jnp.maximum(fold_keys(head, jnp.maximum),
                                           fold_keys(tail, jnp.maximum))

    def accumulate(j, slot, bias_tile):
        start = pl.multiple_of(j * tk, tk)
        v_t = jnp.transpose(v_ref[pl.ds(start, tk), :])
        for mp in range(2):
            t = t_ref[slot, mp]
            if bias_tile is None:
                pm = pm_ref[slot, mp]
            else:
                t = t + bias_ref[mp, bias_tile]
                pm = fold_keys(t, jnp.maximum)
            m_prev = m_ref[mp]
            m_new = jnp.maximum(m_prev, jnp.max(pm, axis=0, keepdims=True))
            p = jnp.exp2(t - m_new)
            a = jnp.exp2(m_prev - m_new)
            l_ref[mp] = a * l_ref[mp] + fold_keys(p, jnp.add)
            m_ref[mp] = m_new
            pv = jnp.dot(v_t, p.astype(BF16), preferred_element_type=F32)
            acc_ref[mp] = a * acc_ref[mp] + pv

    def pipelined(j, count):
        for n in range(count):
            scores(j + n + 1, (n + 1) % 2)
            accumulate(j + n, n % 2, None)

    def prepare(src_ref, q_index):
        for mp in range(2):
            q = src_ref[:, mp * dh:(mp + 1) * dh].astype(F32)
            qt_ref[mp] = jnp.transpose(q).astype(BF16)
        scores(0, 0, q_index)

    @pl.when(qi == 0)
    def _():
        prepare(q_ref, qi)

    quads = qi // 4
    rem = qi % 4

    @pl.loop(0, quads)
    def _(i):
        pipelined(4 * i, 4)

    @pl.when(rem >= 2)
    def _():
        pipelined(4 * quads, 2)

    @pl.when(rem % 2 == 0)
    def _():
        accumulate(qi, 0, 0)

    @pl.when(rem % 2 == 1)
    def _():
        scores(qi, 1)
        accumulate(qi - 1, 0, None)
        accumulate(qi, 1, 0)

    prepare(qn_ref, qi + 1)
    outs = []
    for mp in range(2):
        l = jnp.sum(l_ref[mp], axis=0, keepdims=True)
        outs.append(acc_ref[mp] / l)
    sums = jnp.sum(lq_ref[...] * lk_ref[...], axis=-1, keepdims=True)
    ex = jnp.exp(sums)
    lam = ex[0:1, :] - ex[1:2, :] + lam_init
    o = jnp.transpose(outs[0] - lam * outs[1])
    o = o * lax.rsqrt(jnp.mean(o * o, axis=-1, keepdims=True) + LN_EPS) * g_ref[...]
    o_ref[...] = (o * (1.0 - lam_init)).astype(o_ref.dtype)


def _diff_attention(qkv, bias_tiles, lam_q, lam_k, subln_g, layer, lam_init, bsz, seq, tq, tk):
    m = qkv.shape[0]
    hd = 2 * DIFF_HEAD_DIM
    nq = seq // tq
    assert tq == tk, "the block schedule assumes square score tiles"
    r = tq // tk
    diag, corner = bias_tiles
    c = MAX_DISTANCE
    return pl.pallas_call(
        functools.partial(_diff_attn_kernel, lam_init=lam_init, tq=tq, tk=tk),
        grid=(bsz, DIFF_HEADS, nq),
        in_specs=[
            pl.BlockSpec((tq, hd), lambda b, h, i: (b * nq + i, h)),
            pl.BlockSpec((tq, hd), lambda b, h, i: (b * nq + jnp.minimum(i + 1, nq - 1), h)),
            pl.BlockSpec((seq, hd), lambda b, h, i: (b, DIFF_HEADS + h)),
            pl.BlockSpec((seq, hd), lambda b, h, i: (b, 2 * DIFF_HEADS + h)),
            pl.BlockSpec((None, 2, r, tk, tq), lambda b, h, i: (h, 0, 0, 0, 0)),
            pl.BlockSpec((None, 2, c, c), lambda b, h, i: (h, 0, 0, 0)),
            pl.BlockSpec((None, 2, DIFF_HEAD_DIM), lambda b, h, i: (layer, 0, 0)),
            pl.BlockSpec((None, 2, DIFF_HEAD_DIM), lambda b, h, i: (layer, 0, 0)),
            pl.BlockSpec((None, 1, hd), lambda b, h, i: (layer, 0, 0)),
        ],
        out_specs=pl.BlockSpec((tq, hd), lambda b, h, i: (b * nq + i, h)),
        out_shape=jax.ShapeDtypeStruct((m, DIFF_WIDTH), BF16),
        scratch_shapes=[pltpu.VMEM((2, hd, tq), F32),
                        pltpu.VMEM((2, 1, tq), F32),
                        pltpu.VMEM((2, V7X_SUBLANES, tq), F32),
                        pltpu.VMEM((2, 2, tk, tq), F32),
                        pltpu.VMEM((2, 2, V7X_SUBLANES, tq), F32),
                        pltpu.VMEM((2, DIFF_HEAD_DIM, tq), BF16)],
        compiler_params=_params("parallel", "parallel", "arbitrary"),
        name="diff_attention",
    )(qkv, qkv, qkv, qkv, diag, corner, lam_q, lam_k, subln_g.reshape(-1, 1, hd))


def _mem_attn_kernel(q_ref, kv_ref, o_ref):
    d = MEM_HEAD_DIM
    scale = d ** -0.5
    for h in range(MEM_HEADS):
        q = q_ref[:, h * d:(h + 1) * d]
        k = kv_ref[:, h * d:(h + 1) * d]
        v = kv_ref[:, D_MODEL + h * d:D_MODEL + (h + 1) * d]
        s = lax.dot_general(q, k, (((1,), (1,)), ((), ())), preferred_element_type=F32) * scale
        s = s - jnp.max(s, axis=-1, keepdims=True)
        p = jnp.exp(s)
        p = p / jnp.sum(p, axis=-1, keepdims=True)
        o_ref[:, h * d:(h + 1) * d] = jnp.dot(p.astype(BF16), v,
                                              preferred_element_type=F32).astype(o_ref.dtype)


def _mem_attention(q, kv, seq, tm):
    m = q.shape[0]
    tiles_per_seq = seq // tm
    return pl.pallas_call(
        _mem_attn_kernel,
        grid=(m // tm,),
        in_specs=[pl.BlockSpec((tm, D_MODEL), lambda i: (i, 0)),
                  pl.BlockSpec((N_MEM, 2 * D_MODEL), lambda i: (i // tiles_per_seq, 0))],
        out_specs=pl.BlockSpec((tm, D_MODEL), lambda i: (i, 0)),
        out_shape=jax.ShapeDtypeStruct((m, D_MODEL), BF16),
        compiler_params=_params("parallel"),
        name="mem_attention",
    )(q, kv)


def _ffn_kernel(xb_ref, wg_ref, wu_ref, wd_ref, res_ref, g_ref, b_ref, o_ref, ob_ref, acc_ref):
    f = pl.program_id(1)

    def slab(first):
        xb = xb_ref[...]
        gate = jnp.dot(xb, wg_ref[...], preferred_element_type=F32)
        up = jnp.dot(xb, wu_ref[...], preferred_element_type=F32)
        hid = (gate * (1.0 / (1.0 + jnp.exp(-gate))) * up).astype(BF16)
        part = jnp.dot(hid, wd_ref[...], preferred_element_type=F32)
        acc_ref[...] = part if first else acc_ref[...] + part

    @pl.when(f == 0)
    def _():
        slab(True)

    @pl.when(f > 0)
    def _():
        slab(False)

    @pl.when(f == pl.num_programs(1) - 1)
    def _():
        y = _layer_norm_rows(ALPHA * res_ref[...] + acc_ref[...], g_ref[...], b_ref[...])
        o_ref[...] = y
        ob_ref[...] = y.astype(BF16)


def _ffn(xb, w_gu, w_down, layer, res, g, b, tm, tf):
    m, d = xb.shape
    nf = D_FF // tf
    return pl.pallas_call(
        _ffn_kernel,
        grid=(m // tm, nf),
        in_specs=[pl.BlockSpec((tm, d), lambda i, f: (i, 0)),
                  pl.BlockSpec((None, d, tf), lambda i, f: (layer, 0, f)),
                  pl.BlockSpec((None, d, tf), lambda i, f: (layer, 0, nf + f)),
                  pl.BlockSpec((None, tf, d), lambda i, f: (layer, f, 0)),
                  pl.BlockSpec((tm, d), lambda i, f: (i, 0)),
                  pl.BlockSpec((None, 1, d), lambda i, f: (layer, 0, 0)),
                  pl.BlockSpec((None, 1, d), lambda i, f: (layer, 0, 0))],
        out_specs=[pl.BlockSpec((tm, d), lambda i, f: (i, 0)),
                   pl.BlockSpec((tm, d), lambda i, f: (i, 0))],
        out_shape=[jax.ShapeDtypeStruct((m, d), F32),
                   jax.ShapeDtypeStruct((m, d), BF16)],
        scratch_shapes=[pltpu.VMEM((tm, d), F32)],
        compiler_params=_params("parallel", "arbitrary"),
        name="ffn",
    )(xb, w_gu, w_gu, w_down, res, g.reshape(-1, 1, d), b.reshape(-1, 1, d))


ATTN_TQ = 512
ATTN_TK = 512


def kernel(x, mem, rel_bias, w_in, conv_w, gmlp_ln_g, gmlp_ln_b, gmlp_ws, gmlp_bs,
           diff_lambda_q, diff_lambda_k, diff_subln_g, w_mix_out, ln_mix_g, ln_mix_b,
           w_mem_q, w_mem_kv, w_mem_out, ln_mem_g, ln_mem_b,
           w_ffn_gu, w_ffn_down, ln_ffn_g, ln_ffn_b):
    bsz, seq, d = x.shape
    m = bsz * seq
    x = x.reshape(m, d)
    xb = x.astype(BF16)
    memb = mem.reshape(bsz * N_MEM, d).astype(BF16)
    bias_tiles = _near_bias_tiles(rel_bias, ATTN_TQ, ATTN_TK)
    w_in, w_mix_out, w_mem_q, w_mem_kv, w_mem_out, w_ffn_gu, w_ffn_down = (
        w.astype(BF16) for w in (w_in, w_mix_out, w_mem_q, w_mem_kv, w_mem_out,
                                 w_ffn_gu, w_ffn_down))
    gmlp_bs_t = jnp.transpose(gmlp_bs, (0, 2, 1))
    for l in range(DEPTH):
        lam_init = 0.8 - 0.6 * math.exp(-0.3 * l)
        proj_ab = _matmul(xb, w_in, l, 0, AB_COLS, F32, tm=2048, tn=512)
        qkv = _matmul(xb, w_in, l, AB_COLS, QKV_COLS, BF16, tm=2048, tn=512,
                      scaled_cols=DIFF_WIDTH, scale=ATTN_LOGIT_SCALE)
        y_ab = _mixer_ab(proj_ab, conv_w, gmlp_ln_g, gmlp_ln_b, gmlp_ws, gmlp_bs_t, l, seq, tm=256)
        y_c = _diff_attention(qkv, bias_tiles, diff_lambda_q, diff_lambda_k, diff_subln_g, l,
                              lam_init, bsz, seq, ATTN_TQ, ATTN_TK)
        x, xb = _matmul_res_ln([y_ab, y_c], w_mix_out, l, x, ln_mix_g, ln_mix_b, tm=512)

        q_mem = _matmul(xb, w_mem_q, l, 0, D_MODEL, BF16, tm=2048, tn=1024)
        kv_mem = _matmul(memb, w_mem_kv, l, 0, 2 * D_MODEL, BF16, tm=bsz * N_MEM, tn=1024)
        o_mem = _mem_attention(q_mem, kv_mem, seq, tm=512)
        x, xb = _matmul_res_ln([o_mem], w_mem_out, l, x, ln_mem_g, ln_mem_b, tm=512)

        x, xb = _ffn(xb, w_ffn_gu, w_ffn_down, l, x, ln_ffn_g, ln_ffn_b, tm=512, tf=512)
    return x.reshape(bsz, seq, d)
```

```python
import functools
import math

import jax
import jax.numpy as jnp
from jax import lax
from jax.experimental import pallas as pl
from jax.experimental.pallas import tpu as pltpu

D_MODEL = 2048
DEPTH = 2
N_MEM = 256
MEM_HEADS = 4
MEM_HEAD_DIM = D_MODEL // MEM_HEADS
CONV_WIDTH = D_MODEL // 4
CONV_K = 3
GMLP_WIDTH = D_MODEL // 4
GMLP_GROUPS = 4
GMLP_GROUP_DIM = GMLP_WIDTH // GMLP_GROUPS
CHUNK = 128
DIFF_HEAD_DIM = 128
DIFF_WIDTH = D_MODEL // 2
DIFF_HEADS = DIFF_WIDTH // (2 * DIFF_HEAD_DIM)
AB_COLS = 3 * CONV_WIDTH + 2 * GMLP_WIDTH
QKV_COLS = 3 * DIFF_WIDTH
NUM_BUCKETS = 32
MAX_DISTANCE = 128
D_FF = -(-8 * D_MODEL // (3 * 256)) * 256
ALPHA = (2 * DEPTH) ** 0.25
LN_EPS = 1e-5

V7X_VMEM_BYTES = 64 * 1024 * 1024
V7X_LANES = 128
V7X_SUBLANES = 8
VMEM_LIMIT_BYTES = V7X_VMEM_BYTES - 8 * 1024 * 1024

MASK_VALUE = -1e30
LOG2E = math.log2(math.e)
ATTN_LOGIT_SCALE = DIFF_HEAD_DIM ** -0.5 * LOG2E

F32 = jnp.float32
BF16 = jnp.bfloat16


def _params(*semantics):
    return pltpu.CompilerParams(dimension_semantics=semantics,
                                vmem_limit_bytes=VMEM_LIMIT_BYTES)


def _layer_norm_rows(x, g, b):
    mu = jnp.mean(x, axis=-1, keepdims=True)
    xc = x - mu
    var = jnp.mean(xc * xc, axis=-1, keepdims=True)
    return xc * lax.rsqrt(var + LN_EPS) * g + b


def _mm_kernel(a_ref, w_ref, o_ref, *, scaled_tiles, scale):
    acc = jnp.dot(a_ref[...], w_ref[...], preferred_element_type=F32)
    if scaled_tiles:
        acc = acc * jnp.where(pl.program_id(1) < scaled_tiles, scale, 1.0)
    o_ref[...] = acc.astype(o_ref.dtype)


def _matmul(a, w, layer, col0, n, out_dtype, tm, tn, scaled_cols=0, scale=1.0):
    m, k = a.shape
    j0 = col0 // tn
    return pl.pallas_call(
        functools.partial(_mm_kernel, scaled_tiles=scaled_cols // tn, scale=scale),
        grid=(m // tm, n // tn),
        in_specs=[pl.BlockSpec((tm, k), lambda i, j: (i, 0)),
                  pl.BlockSpec((None, k, tn), lambda i, j: (layer, 0, j0 + j))],
        out_specs=pl.BlockSpec((tm, tn), lambda i, j: (i, j)),
        out_shape=jax.ShapeDtypeStruct((m, n), out_dtype),
        compiler_params=_params("parallel", "parallel"),
        name="matmul",
    )(a, w)


def _mm_res_ln_kernel(*refs, n_a):
    a_refs = refs[:n_a]
    w_ref, res_ref, g_ref, b_ref, o_ref, ob_ref = refs[n_a:]
    a = jnp.concatenate([a_ref[...] for a_ref in a_refs], axis=1)
    acc = jnp.dot(a, w_ref[...], preferred_element_type=F32)
    y = _layer_norm_rows(ALPHA * res_ref[...] + acc, g_ref[...], b_ref[...])
    o_ref[...] = y
    ob_ref[...] = y.astype(BF16)


def _matmul_res_ln(a_list, w, layer, res, g, b, tm):
    m = res.shape[0]
    _, k, n = w.shape
    in_specs = [pl.BlockSpec((tm, a.shape[1]), lambda i: (i, 0)) for a in a_list]
    in_specs += [pl.BlockSpec((None, k, n), lambda i: (layer, 0, 0)),
                 pl.BlockSpec((tm, n), lambda i: (i, 0)),
                 pl.BlockSpec((None, 1, n), lambda i: (layer, 0, 0)),
                 pl.BlockSpec((None, 1, n), lambda i: (layer, 0, 0))]
    return pl.pallas_call(
        functools.partial(_mm_res_ln_kernel, n_a=len(a_list)),
        grid=(m // tm,),
        in_specs=in_specs,
        out_specs=[pl.BlockSpec((tm, n), lambda i: (i, 0)),
                   pl.BlockSpec((tm, n), lambda i: (i, 0))],
        out_shape=[jax.ShapeDtypeStruct((m, n), F32),
                   jax.ShapeDtypeStruct((m, n), BF16)],
        compiler_params=_params("parallel"),
        name="matmul_res_ln",
    )(*a_list, w, res, g.reshape(-1, 1, n), b.reshape(-1, 1, n))


def _mixer_ab_kernel(bg_ref, cg_ref, h_ref, u_ref, v_ref, cgp_ref, hp_ref,
                     cw_ref, lg_ref, lb_ref, ws_ref, bst_ref, o_ref, *, tm, tiles_per_seq):
    i = pl.program_id(0)
    z = cg_ref[...] * h_ref[...]
    zp = cgp_ref[...] * hp_ref[...]
    zp = jnp.where(i % tiles_per_seq == 0, 0.0, zp)
    rows = lax.broadcasted_iota(jnp.int32, (tm, 1), 0)
    last = zp[V7X_SUBLANES - 1:V7X_SUBLANES, :]
    last2 = zp[V7X_SUBLANES - 2:V7X_SUBLANES - 1, :]
    z1 = jnp.where(rows == 0, last, pltpu.roll(z, 1, axis=0))
    z2 = jnp.where(rows == 0, last2, jnp.where(rows == 1, last, pltpu.roll(z, 2, axis=0)))
    cw = cw_ref[...]
    conv = cw[0:1, :] * z2 + cw[1:2, :] * z1 + cw[2:3, :] * z
    o_ref[:, 0:CONV_WIDTH] = (bg_ref[...] * conv).astype(o_ref.dtype)

    u = jax.nn.gelu(u_ref[...])
    vn = _layer_norm_rows(jax.nn.gelu(v_ref[...]), lg_ref[...], lb_ref[...]).astype(BF16)
    tr = lax.broadcasted_iota(jnp.int32, (CHUNK, CHUNK), 0)
    tc = lax.broadcasted_iota(jnp.int32, (CHUNK, CHUNK), 1)
    causal = tr >= tc
    bst = bst_ref[...]
    for g in range(GMLP_GROUPS):
        wg = jnp.where(causal, ws_ref[g], 0.0).astype(BF16)
        bias = bst[:, g:g + 1]
        c0 = g * GMLP_GROUP_DIM
        for c in range(tm // CHUNK):
            r0 = c * CHUNK
            sv = jnp.dot(wg, vn[r0:r0 + CHUNK, c0:c0 + GMLP_GROUP_DIM],
                         preferred_element_type=F32) + bias
            o_ref[r0:r0 + CHUNK, CONV_WIDTH + c0:CONV_WIDTH + c0 + GMLP_GROUP_DIM] = (
                u[r0:r0 + CHUNK, c0:c0 + GMLP_GROUP_DIM] * sv).astype(o_ref.dtype)


def _mixer_ab(proj_ab, conv_w, ln_g, ln_b, ws, bs_t, layer, seq, tm):
    m = proj_ab.shape[0]
    w = CONV_WIDTH
    halo_blocks = tm // V7X_SUBLANES

    def col(c):
        return pl.BlockSpec((tm, w), lambda i, c=c: (i, c))

    def halo(c):
        return pl.BlockSpec((V7X_SUBLANES, w),
                            lambda i, c=c: (jnp.maximum(i * halo_blocks - 1, 0), c))

    def whole(shape):
        return pl.BlockSpec((None,) + shape, lambda i: (layer,) + (0,) * len(shape))

    return pl.pallas_call(
        functools.partial(_mixer_ab_kernel, tm=tm, tiles_per_seq=seq // tm),
        grid=(m // tm,),
        in_specs=[col(0), col(1), col(2), col(3), col(4), halo(1), halo(2),
                  whole((CONV_K, w)), whole((1, w)), whole((1, w)),
                  whole((GMLP_GROUPS, CHUNK, CHUNK)), whole((CHUNK, GMLP_GROUPS))],
        out_specs=pl.BlockSpec((tm, 2 * w), lambda i: (i, 0)),
        out_shape=jax.ShapeDtypeStruct((m, 2 * w), BF16),
        compiler_params=_params("parallel"),
        name="mixer_ab",
    )(proj_ab, proj_ab, proj_ab, proj_ab, proj_ab, proj_ab, proj_ab,
      conv_w, ln_g.reshape(-1, 1, w), ln_b.reshape(-1, 1, w), ws, bs_t)


def _causal_bucket(n):
    max_exact = NUM_BUCKETS // 2
    nf = jnp.maximum(n, 1).astype(F32)
    large = max_exact + (jnp.log(nf / max_exact) / math.log(MAX_DISTANCE / max_exact)
                         * (NUM_BUCKETS - max_exact)).astype(jnp.int32)
    large = jnp.minimum(large, NUM_BUCKETS - 1)
    return jnp.where(n < max_exact, n, large)


def _near_bias_tiles(rel_bias, tq, tk):
    r = tq // tk
    rb = rel_bias.astype(F32)
    rb = jnp.transpose((rb - rb[NUM_BUCKETS - 1]) * LOG2E).reshape(DIFF_HEADS, 2, NUM_BUCKETS)

    def lookup(dist):
        bucket = _causal_bucket(jnp.maximum(dist, 0))
        tiles = jnp.zeros((DIFF_HEADS, 2) + dist.shape, F32)
        for b in range(NUM_BUCKETS - 1):
            tiles = jnp.where(bucket == b, rb[:, :, b].reshape((DIFF_HEADS, 2) + (1,) * dist.ndim),
                              tiles)
        return jnp.where(dist >= 0, tiles, MASK_VALUE)

    qpos = jnp.arange(tq)[None, :]
    kpos = jnp.arange(tk)[:, None]
    diag = lookup(jnp.stack([qpos - kpos - e * tk for e in range(r)], axis=0))
    c = MAX_DISTANCE
    corner = lookup(jnp.arange(c)[None, :] - jnp.arange(c)[:, None] + c)
    return diag, corner


def _diff_attn_kernel(q_ref, qn_ref, k_ref, v_ref, bias_ref, corner_ref, lq_ref, lk_ref, g_ref, o_ref,
                      acc_ref, m_ref, l_ref, t_ref, pm_ref, qt_ref, *, lam_init, tq, tk):
    qi = pl.program_id(2)
    dh = DIFF_HEAD_DIM
    sub = V7X_SUBLANES
    acc_ref[...] = jnp.zeros_like(acc_ref)
    m_ref[...] = jnp.full_like(m_ref, MASK_VALUE)
    l_ref[...] = jnp.zeros_like(l_ref)

    def fold_keys(x, op):
        out = x[0:sub]
        for i in range(1, x.shape[0] // sub):
            out = op(out, x[i * sub:(i + 1) * sub])
        return out

    def scores(j, slot, q_index=qi):
        start = pl.multiple_of(j * tk, tk)
        is_below = (j == q_index - 1).astype(F32)
        c = MAX_DISTANCE
        for mp in range(2):
            k = k_ref[pl.ds(start, tk), mp * dh:(mp + 1) * dh]
            t = jnp.dot(k, qt_ref[mp], preferred_element_type=F32)
            head, tail = t[0:tk - c], t[tk - c:tk]
            tail = jnp.concatenate([tail[:, 0:c] + is_below * corner_ref[mp], tail[:, c:]], axis=1)
            t_ref[slot, mp, 0:tk - c, :] = head
            t_ref[slot, mp, tk - c:tk, :] = tail
            pm_ref[slot, mp] = jnp.maximum(fold_keys(head, jnp.maximum),
                                           fold_keys(tail, jnp.maximum))

    def accumulate(j, slot, bias_tile):
        start = pl.multiple_of(j * tk, tk)
        v_t = jnp.transpose(v_ref[pl.ds(start, tk), :])
        for mp in range(2):
            t = t_ref[slot, mp]
            if bias_tile is None:
                pm = pm_ref[slot, mp]
            else:
                t = t + bias_ref[mp, bias_tile]
                pm = fold_keys(t, jnp.maximum)
            m_prev = m_ref[mp]
            m_new = jnp.maximum(m_prev, jnp.max(pm, axis=0, keepdims=True))
            p = jnp.exp2(t - m_new)
            a = jnp.exp2(m_prev - m_new)
            l_ref[mp] = a * l_ref[mp] + fold_keys(p, jnp.add)
            m_ref[mp] = m_new
            pv = jnp.dot(v_t, p.astype(BF16), preferred_element_type=F32)
            acc_ref[mp] = a * acc_ref[mp] + pv

    def pipelined(j, count):
        for n in range(count):
            scores(j + n + 1, (n + 1) % 2)
            accumulate(j + n, n % 2, None)

    def prepare(src_ref, q_index):
        for mp in range(2):
            q = src_ref[:, mp * dh:(mp + 1) * dh].astype(F32)
            qt_ref[mp] = jnp.transpose(q).astype(BF16)
        scores(0, 0, q_index)

    @pl.when(qi == 0)
    def _():
        prepare(q_ref, qi)

    quads = qi // 4
    rem = qi % 4

    @pl.loop(0, quads)
    def _(i):
        pipelined(4 * i, 4)

    @pl.when(rem >= 2)
    def _():
        pipelined(4 * quads, 2)

    @pl.when(rem % 2 == 0)
    def _():
        accumulate(qi, 0, 0)

    @pl.when(rem % 2 == 1)
    def _():
        scores(qi, 1)
        accumulate(qi - 1, 0, None)
        accumulate(qi, 1, 0)

    prepare(qn_ref, qi + 1)
    outs = []
    for mp in range(2):
        l = jnp.sum(l_ref[mp], axis=0, keepdims=True)
        outs.append(acc_ref[mp] / l)
    sums = jnp.sum(lq_ref[...] * lk_ref[...], axis=-1, keepdims=True)
    ex = jnp.exp(sums)
    lam = ex[0:1, :] - ex[1:2, :] + lam_init
    o = jnp.transpose(outs[0] - lam * outs[1])
    o = o * lax.rsqrt(jnp.mean(o * o, axis=-1, keepdims=True) + LN_EPS) * g_ref[...]
    o_ref[...] = (o * (1.0 - lam_init)).astype(o_ref.dtype)


def _diff_attention(qkv, bias_tiles, lam_q, lam_k, subln_g, layer, lam_init, bsz, seq, tq, tk):
    m = qkv.shape[0]
    hd = 2 * DIFF_HEAD_DIM
    nq = seq // tq
    assert tq == tk, "the block schedule assumes square score tiles"
    r = tq // tk
    diag, corner = bias_tiles
    c = MAX_DISTANCE
    return pl.pallas_call(
        functools.partial(_diff_attn_kernel, lam_init=lam_init, tq=tq, tk=tk),
        grid=(bsz, DIFF_HEADS, nq),
        in_specs=[
            pl.BlockSpec((tq, hd), lambda b, h, i: (b * nq + i, h)),
            pl.BlockSpec((tq, hd), lambda b, h, i: (b * nq + jnp.minimum(i + 1, nq - 1), h)),
            pl.BlockSpec((seq, hd), lambda b, h, i: (b, DIFF_HEADS + h)),
            pl.BlockSpec((seq, hd), lambda b, h, i: (b, 2 * DIFF_HEADS + h)),
            pl.BlockSpec((None, 2, r, tk, tq), lambda b, h, i: (h, 0, 0, 0, 0)),
            pl.BlockSpec((None, 2, c, c), lambda b, h, i: (h, 0, 0, 0)),
            pl.BlockSpec((None, 2, DIFF_HEAD_DIM), lambda b, h, i: (layer, 0, 0)),
            pl.BlockSpec((None, 2, DIFF_HEAD_DIM), lambda b, h, i: (layer, 0, 0)),
            pl.BlockSpec((None, 1, hd), lambda b, h, i: (layer, 0, 0)),
        ],
        out_specs=pl.BlockSpec((tq, hd), lambda b, h, i: (b * nq + i, h)),
        out_shape=jax.ShapeDtypeStruct((m, DIFF_WIDTH), BF16),
        scratch_shapes=[pltpu.VMEM((2, hd, tq), F32),
                        pltpu.VMEM((2, 1, tq), F32),
                        pltpu.VMEM((2, V7X_SUBLANES, tq), F32),
                        pltpu.VMEM((2, 2, tk, tq), F32),
                        pltpu.VMEM((2, 2, V7X_SUBLANES, tq), F32),
                        pltpu.VMEM((2, DIFF_HEAD_DIM, tq), BF16)],
        compiler_params=_params("parallel", "parallel", "arbitrary"),
        name="diff_attention",
    )(qkv, qkv, qkv, qkv, diag, corner, lam_q, lam_k, subln_g.reshape(-1, 1, hd))


def _mem_attn_kernel(q_ref, kv_ref, o_ref):
    d = MEM_HEAD_DIM
    scale = d ** -0.5
    for h in range(MEM_HEADS):
        q = q_ref[:, h * d:(h + 1) * d]
        k = kv_ref[:, h * d:(h + 1) * d]
        v = kv_ref[:, D_MODEL + h * d:D_MODEL + (h + 1) * d]
        s = lax.dot_general(q, k, (((1,), (1,)), ((), ())), preferred_element_type=F32) * scale
        s = s - jnp.max(s, axis=-1, keepdims=True)
        p = jnp.exp(s)
        p = p / jnp.sum(p, axis=-1, keepdims=True)
        o_ref[:, h * d:(h + 1) * d] = jnp.dot(p.astype(BF16), v,
                                              preferred_element_type=F32).astype(o_ref.dtype)


def _mem_attention(q, kv, seq, tm):
    m = q.shape[0]
    tiles_per_seq = seq // tm
    return pl.pallas_call(
        _mem_attn_kernel,
        grid=(m // tm,),
        in_specs=[pl.BlockSpec((tm, D_MODEL), lambda i: (i, 0)),
                  pl.BlockSpec((N_MEM, 2 * D_MODEL), lambda i: (i // tiles_per_seq, 0))],
        out_specs=pl.BlockSpec((tm, D_MODEL), lambda i: (i, 0)),
        out_shape=jax.ShapeDtypeStruct((m, D_MODEL), BF16),
        compiler_params=_params("parallel"),
        name="mem_attention",
    )(q, kv)


def _ffn_kernel(xb_ref, wg_ref, wu_ref, wd_ref, res_ref, g_ref, b_ref, o_ref, ob_ref, acc_ref):
    f = pl.program_id(1)

    def slab(first):
        xb = xb_ref[...]
        gate = jnp.dot(xb, wg_ref[...], preferred_element_type=F32)
        up = jnp.dot(xb, wu_ref[...], preferred_element_type=F32)
        hid = (gate * (1.0 / (1.0 + jnp.exp(-gate))) * up).astype(BF16)
        part = jnp.dot(hid, wd_ref[...], preferred_element_type=F32)
        acc_ref[...] = part if first else acc_ref[...] + part

    @pl.when(f == 0)
    def _():
        slab(True)

    @pl.when(f > 0)
    def _():
        slab(False)

    @pl.when(f == pl.num_programs(1) - 1)
    def _():
        y = _layer_norm_rows(ALPHA * res_ref[...] + acc_ref[...], g_ref[...], b_ref[...])
        o_ref[...] = y
        ob_ref[...] = y.astype(BF16)


def _ffn(xb, w_gu, w_down, layer, res, g, b, tm, tf):
    m, d = xb.shape
    nf = D_FF // tf
    return pl.pallas_call(
        _ffn_kernel,
        grid=(m // tm, nf),
        in_specs=[pl.BlockSpec((tm, d), lambda i, f: (i, 0)),
                  pl.BlockSpec((None, d, tf), lambda i, f: (layer, 0, f)),
                  pl.BlockSpec((None, d, tf), lambda i, f: (layer, 0, nf + f)),
                  pl.BlockSpec((None, tf, d), lambda i, f: (layer, f, 0)),
                  pl.BlockSpec((tm, d), lambda i, f: (i, 0)),
                  pl.BlockSpec((None, 1, d), lambda i, f: (layer, 0, 0)),
                  pl.BlockSpec((None, 1, d), lambda i, f: (layer, 0, 0))],
        out_specs=[pl.BlockSpec((tm, d), lambda i, f: (i, 0)),
                   pl.BlockSpec((tm, d), lambda i, f: (i, 0))],
        out_shape=[jax.ShapeDtypeStruct((m, d), F32),
                   jax.ShapeDtypeStruct((m, d), BF16)],
        scratch_shapes=[pltpu.VMEM((tm, d), F32)],
        compiler_params=_params("parallel", "arbitrary"),
        name="ffn",
    )(xb, w_gu, w_gu, w_down, res, g.reshape(-1, 1, d), b.reshape(-1, 1, d))


class _Tiles:
    proj_rows = 2048
    proj_cols = 512
    mem_q_cols = 1024
    mem_kv_cols = 1024
    res_ln_rows = 512
    mixer_rows = 512
    mem_attn_rows = 512
    ffn_rows = 512
    ffn_slab = 512
    attn = 512


def kernel(x, mem, rel_bias, w_in, conv_w, gmlp_ln_g, gmlp_ln_b, gmlp_ws, gmlp_bs,
           diff_lambda_q, diff_lambda_k, diff_subln_g, w_mix_out, ln_mix_g, ln_mix_b,
           w_mem_q, w_mem_kv, w_mem_out, ln_mem_g, ln_mem_b,
           w_ffn_gu, w_ffn_down, ln_ffn_g, ln_ffn_b):
    bsz, seq, d = x.shape
    m = bsz * seq
    x = x.reshape(m, d)
    xb = x.astype(BF16)
    memb = mem.reshape(bsz * N_MEM, d).astype(BF16)
    t = _Tiles
    bias_tiles = _near_bias_tiles(rel_bias, t.attn, t.attn)
    w_in, w_mix_out, w_mem_q, w_mem_kv, w_mem_out, w_ffn_gu, w_ffn_down = (
        w.astype(BF16) for w in (w_in, w_mix_out, w_mem_q, w_mem_kv, w_mem_out,
                                 w_ffn_gu, w_ffn_down))
    gmlp_bs_t = jnp.transpose(gmlp_bs, (0, 2, 1))
    for l in range(DEPTH):
        lam_init = 0.8 - 0.6 * math.exp(-0.3 * l)
        proj_ab = _matmul(xb, w_in, l, 0, AB_COLS, F32, t.proj_rows, t.proj_cols)
        qkv = _matmul(xb, w_in, l, AB_COLS, QKV_COLS, BF16, t.proj_rows, t.proj_cols,
                      scaled_cols=DIFF_WIDTH, scale=ATTN_LOGIT_SCALE)
        y_ab = _mixer_ab(proj_ab, conv_w, gmlp_ln_g, gmlp_ln_b, gmlp_ws, gmlp_bs_t, l, seq,
                         t.mixer_rows)
        y_c = _diff_attention(qkv, bias_tiles, diff_lambda_q, diff_lambda_k, diff_subln_g, l,
                              lam_init, bsz, seq, t.attn, t.attn)
        x, xb = _matmul_res_ln([y_ab, y_c], w_mix_out, l, x, ln_mix_g, ln_mix_b, t.res_ln_rows)

        q_mem = _matmul(xb, w_mem_q, l, 0, D_MODEL, BF16, t.proj_rows, t.mem_q_cols)
        kv_mem = _matmul(memb, w_mem_kv, l, 0, 2 * D_MODEL, BF16, bsz * N_MEM, t.mem_kv_cols)
        o_mem = _mem_attention(q_mem, kv_mem, seq, t.mem_attn_rows)
        x, xb = _matmul_res_ln([o_mem], w_mem_out, l, x, ln_mem_g, ln_mem_b, t.res_ln_rows)

        x, xb = _ffn(xb, w_ffn_gu, w_ffn_down, l, x, ln_ffn_g, ln_ffn_b, t.ffn_rows, t.ffn_slab)
    return x.reshape(bsz, seq, d)
```

```python
import functools
import math

import jax
import jax.numpy as jnp
from jax import lax
from jax.experimental import pallas as pl
from jax.experimental.pallas import tpu as pltpu

D_MODEL = 2048
DEPTH = 2
N_MEM = 256
MEM_HEADS = 4
MEM_HEAD_DIM = D_MODEL // MEM_HEADS
CONV_WIDTH = D_MODEL // 4
CONV_K = 3
GMLP_WIDTH = D_MODEL // 4
GMLP_GROUPS = 4
GMLP_GROUP_DIM = GMLP_WIDTH // GMLP_GROUPS
CHUNK = 128
DIFF_HEAD_DIM = 128
DIFF_WIDTH = D_MODEL // 2
DIFF_HEADS = DIFF_WIDTH // (2 * DIFF_HEAD_DIM)
AB_COLS = 3 * CONV_WIDTH + 2 * GMLP_WIDTH
QKV_COLS = 3 * DIFF_WIDTH
NUM_BUCKETS = 32
MAX_DISTANCE = 128
D_FF = -(-8 * D_MODEL // (3 * 256)) * 256
ALPHA = (2 * DEPTH) ** 0.25
LN_EPS = 1e-5

V7X_VMEM_BYTES = 64 * 1024 * 1024
V7X_LANES = 128
V7X_SUBLANES = 8
VMEM_LIMIT_BYTES = V7X_VMEM_BYTES - 8 * 1024 * 1024

MASK_VALUE = -1e30
LOG2E = math.log2(math.e)
ATTN_LOGIT_SCALE = DIFF_HEAD_DIM ** -0.5 * LOG2E

F32 = jnp.float32
BF16 = jnp.bfloat16


def _params(*semantics):
    return pltpu.CompilerParams(dimension_semantics=semantics,
                                vmem_limit_bytes=VMEM_LIMIT_BYTES)


def _layer_norm_rows(x, g, b):
    mu = jnp.mean(x, axis=-1, keepdims=True)
    xc = x - mu
    var = jnp.mean(xc * xc, axis=-1, keepdims=True)
    return xc * lax.rsqrt(var + LN_EPS) * g + b


def _mm_kernel(a_ref, w_ref, o_ref, *, scaled_tiles, scale):
    acc = jnp.dot(a_ref[...], w_ref[...], preferred_element_type=F32)
    if scaled_tiles:
        acc = acc * jnp.where(pl.program_id(1) < scaled_tiles, scale, 1.0)
    o_ref[...] = acc.astype(o_ref.dtype)


def _matmul(a, w, layer, col0, n, out_dtype, tm, tn, scaled_cols=0, scale=1.0):
    m, k = a.shape
    j0 = col0 // tn
    return pl.pallas_call(
        functools.partial(_mm_kernel, scaled_tiles=scaled_cols // tn, scale=scale),
        grid=(m // tm, n // tn),
        in_specs=[pl.BlockSpec((tm, k), lambda i, j: (i, 0)),
                  pl.BlockSpec((None, k, tn), lambda i, j: (layer, 0, j0 + j))],
        out_specs=pl.BlockSpec((tm, tn), lambda i, j: (i, j)),
        out_shape=jax.ShapeDtypeStruct((m, n), out_dtype),
        compiler_params=_params("parallel", "parallel"),
        name="matmul",
    )(a, w)


def _mm_res_ln_kernel(*refs, n_a):
    a_refs = refs[:n_a]
    w_ref, res_ref, g_ref, b_ref, o_ref, ob_ref = refs[n_a:]
    a = jnp.concatenate([a_ref[...] for a_ref in a_refs], axis=1)
    acc = jnp.dot(a, w_ref[...], preferred_element_type=F32)
    y = _layer_norm_rows(ALPHA * res_ref[...] + acc, g_ref[...], b_ref[...])
    o_ref[...] = y
    ob_ref[...] = y.astype(BF16)


def _matmul_res_ln(a_list, w, layer, res, g, b, tm):
    m = res.shape[0]
    _, k, n = w.shape
    in_specs = [pl.BlockSpec((tm, a.shape[1]), lambda i: (i, 0)) for a in a_list]
    in_specs += [pl.BlockSpec((None, k, n), lambda i: (layer, 0, 0)),
                 pl.BlockSpec((tm, n), lambda i: (i, 0)),
                 pl.BlockSpec((None, 1, n), lambda i: (layer, 0, 0)),
                 pl.BlockSpec((None, 1, n), lambda i: (layer, 0, 0))]
    return pl.pallas_call(
        functools.partial(_mm_res_ln_kernel, n_a=len(a_list)),
        grid=(m // tm,),
        in_specs=in_specs,
        out_specs=[pl.BlockSpec((tm, n), lambda i: (i, 0)),
                   pl.BlockSpec((tm, n), lambda i: (i, 0))],
        out_shape=[jax.ShapeDtypeStruct((m, n), F32),
                   jax.ShapeDtypeStruct((m, n), BF16)],
        compiler_params=_params("parallel"),
        name="matmul_res_ln",
    )(*a_list, w, res, g.reshape(-1, 1, n), b.reshape(-1, 1, n))


def _mixer_ab_kernel(bg_ref, cg_ref, h_ref, u_ref, v_ref, cgp_ref, hp_ref,
                     cw_ref, lg_ref, lb_ref, ws_ref, bst_ref, o_ref, *, tm, tiles_per_seq):
    i = pl.program_id(0)
    z = cg_ref[...] * h_ref[...]
    zp = cgp_ref[...] * hp_ref[...]
    zp = jnp.where(i % tiles_per_seq == 0, 0.0, zp)
    rows = lax.broadcasted_iota(jnp.int32, (tm, 1), 0)
    last = zp[V7X_SUBLANES - 1:V7X_SUBLANES, :]
    last2 = zp[V7X_SUBLANES - 2:V7X_SUBLANES - 1, :]
    z1 = jnp.where(rows == 0, last, pltpu.roll(z, 1, axis=0))
    z2 = jnp.where(rows == 0, last2, jnp.where(rows == 1, last, pltpu.roll(z, 2, axis=0)))
    cw = cw_ref[...]
    conv = cw[0:1, :] * z2 + cw[1:2, :] * z1 + cw[2:3, :] * z
    o_ref[:, 0:CONV_WIDTH] = (bg_ref[...] * conv).astype(o_ref.dtype)

    u = jax.nn.gelu(u_ref[...])
    vn = _layer_norm_rows(jax.nn.gelu(v_ref[...]), lg_ref[...], lb_ref[...]).astype(BF16)
    tr = lax.broadcasted_iota(jnp.int32, (CHUNK, CHUNK), 0)
    tc = lax.broadcasted_iota(jnp.int32, (CHUNK, CHUNK), 1)
    causal = tr >= tc
    bst = bst_ref[...]
    for g in range(GMLP_GROUPS):
        wg = jnp.where(causal, ws_ref[g], 0.0).astype(BF16)
        bias = bst[:, g:g + 1]
        c0 = g * GMLP_GROUP_DIM
        for c in range(tm // CHUNK):
            r0 = c * CHUNK
            sv = jnp.dot(wg, vn[r0:r0 + CHUNK, c0:c0 + GMLP_GROUP_DIM],
                         preferred_element_type=F32) + bias
            o_ref[r0:r0 + CHUNK, CONV_WIDTH + c0:CONV_WIDTH + c0 + GMLP_GROUP_DIM] = (
                u[r0:r0 + CHUNK, c0:c0 + GMLP_GROUP_DIM] * sv).astype(o_ref.dtype)


def _mixer_ab(proj_ab, conv_w, ln_g, ln_b, ws, bs_t, layer, seq, tm):
    m = proj_ab.shape[0]
    w = CONV_WIDTH
    halo_blocks = tm // V7X_SUBLANES

    def col(c):
        return pl.BlockSpec((tm, w), lambda i, c=c: (i, c))

    def halo(c):
        return pl.BlockSpec((V7X_SUBLANES, w),
                            lambda i, c=c: (jnp.maximum(i * halo_blocks - 1, 0), c))

    def whole(shape):
        return pl.BlockSpec((None,) + shape, lambda i: (layer,) + (0,) * len(shape))

    return pl.pallas_call(
        functools.partial(_mixer_ab_kernel, tm=tm, tiles_per_seq=seq // tm),
        grid=(m // tm,),
        in_specs=[col(0), col(1), col(2), col(3), col(4), halo(1), halo(2),
                  whole((CONV_K, w)), whole((1, w)), whole((1, w)),
                  whole((GMLP_GROUPS, CHUNK, CHUNK)), whole((CHUNK, GMLP_GROUPS))],
        out_specs=pl.BlockSpec((tm, 2 * w), lambda i: (i, 0)),
        out_shape=jax.ShapeDtypeStruct((m, 2 * w), BF16),
        compiler_params=_params("parallel"),
        name="mixer_ab",
    )(proj_ab, proj_ab, proj_ab, proj_ab, proj_ab, proj_ab, proj_ab,
      conv_w, ln_g.reshape(-1, 1, w), ln_b.reshape(-1, 1, w), ws, bs_t)


def _causal_bucket(n):
    max_exact = NUM_BUCKETS // 2
    nf = jnp.maximum(n, 1).astype(F32)
    large = max_exact + (jnp.log(nf / max_exact) / math.log(MAX_DISTANCE / max_exact)
                         * (NUM_BUCKETS - max_exact)).astype(jnp.int32)
    large = jnp.minimum(large, NUM_BUCKETS - 1)
    return jnp.where(n < max_exact, n, large)


def _near_bias_tiles(rel_bias, tq, tk):
    r = tq // tk
    rb = rel_bias.astype(F32)
    rb = jnp.transpose((rb - rb[NUM_BUCKETS - 1]) * LOG2E).reshape(DIFF_HEADS, 2, NUM_BUCKETS)

    def lookup(dist):
        bucket = _causal_bucket(jnp.maximum(dist, 0))
        tiles = jnp.zeros((DIFF_HEADS, 2) + dist.shape, F32)
        for b in range(NUM_BUCKETS - 1):
            tiles = jnp.where(bucket == b, rb[:, :, b].reshape((DIFF_HEADS, 2) + (1,) * dist.ndim),
                              tiles)
        return jnp.where(dist >= 0, tiles, MASK_VALUE)

    qpos = jnp.arange(tq)[None, :]
    kpos = jnp.arange(tk)[:, None]
    diag = lookup(jnp.stack([qpos - kpos - e * tk for e in range(r)], axis=0))
    c = MAX_DISTANCE
    corner = lookup(jnp.arange(c)[None, :] - jnp.arange(c)[:, None] + c)
    return diag, corner


def _diff_attn_kernel(q_ref, qn_ref, k_ref, v_ref, bias_ref, corner_ref, lq_ref, lk_ref, g_ref, o_ref,
                      acc_ref, m_ref, l_ref, t_ref, pm_ref, qt_ref, *, lam_init, tq, tk):
    qi = pl.program_id(2)
    dh = DIFF_HEAD_DIM
    sub = V7X_SUBLANES
    acc_ref[...] = jnp.zeros_like(acc_ref)
    m_ref[...] = jnp.full_like(m_ref, MASK_VALUE)
    l_ref[...] = jnp.zeros_like(l_ref)

    def fold_keys(x, op):
        out = x[0:sub]
        for i in range(1, x.shape[0] // sub):
            out = op(out, x[i * sub:(i + 1) * sub])
        return out

    def scores(j, slot, q_index=qi):
        start = pl.multiple_of(j * tk, tk)
        is_below = (j == q_index - 1).astype(F32)
        c = MAX_DISTANCE
        for mp in range(2):
            k = k_ref[pl.ds(start, tk), mp * dh:(mp + 1) * dh]
            t = jnp.dot(k, qt_ref[mp], preferred_element_type=F32)
            head, tail = t[0:tk - c], t[tk - c:tk]
            tail = jnp.concatenate([tail[:, 0:c] + is_below * corner_ref[mp], tail[:, c:]], axis=1)
            t_ref[slot, mp, 0:tk - c, :] = head
            t_ref[slot, mp, tk - c:tk, :] = tail
            pm_ref[slot, mp] = jnp.maximum(fold_keys(head, jnp.maximum),
                                           fold_keys(tail, jnp.maximum))

    def accumulate(j, slot, bias_tile):
        start = pl.multiple_of(j * tk, tk)
        v_t = jnp.transpose(v_ref[pl.ds(start, tk), :])
        for mp in range(2):
            t = t_ref[slot, mp]
            if bias_tile is None:
                pm = pm_ref[slot, mp]
            else:
                t = t + bias_ref[mp, bias_tile]
                pm = fold_keys(t, jnp.maximum)
            m_prev = m_ref[mp]
            m_new = jnp.maximum(m_prev, jnp.max(pm, axis=0, keepdims=True))
            p = jnp.exp2(t - m_new)
            a = jnp.exp2(m_prev - m_new)
            l_ref[mp] = a * l_ref[mp] + fold_keys(p, jnp.add)
            m_ref[mp] = m_new
            pv = jnp.dot(v_t, p.astype(BF16), preferred_element_type=F32)
            acc_ref[mp] = a * acc_ref[mp] + pv

    def pipelined(j, count):
        for n in range(count):
            scores(j + n + 1, (n + 1) % 2)
            accumulate(j + n, n % 2, None)

    def prepare(src_ref, q_index):
        for mp in range(2):
            q = src_ref[:, mp * dh:(mp + 1) * dh].astype(F32)
            qt_ref[mp] = jnp.transpose(q).astype(BF16)
        scores(0, 0, q_index)

    @pl.when(qi == 0)
    def _():
        prepare(q_ref, qi)

    quads = qi // 4
    rem = qi % 4

    @pl.loop(0, quads)
    def _(i):
        pipelined(4 * i, 4)

    @pl.when(rem >= 2)
    def _():
        pipelined(4 * quads, 2)

    @pl.when(rem % 2 == 0)
    def _():
        accumulate(qi, 0, 0)

    @pl.when(rem % 2 == 1)
    def _():
        scores(qi, 1)
        accumulate(qi - 1, 0, None)
        accumulate(qi, 1, 0)

    prepare(qn_ref, qi + 1)
    outs = []
    for mp in range(2):
        l = jnp.sum(l_ref[mp], axis=0, keepdims=True)
        outs.append(acc_ref[mp] / l)
    sums = jnp.sum(lq_ref[...] * lk_ref[...], axis=-1, keepdims=True)
    ex = jnp.exp(sums)
    lam = ex[0:1, :] - ex[1:2, :] + lam_init
    o = jnp.transpose(outs[0] - lam * outs[1])
    o = o * lax.rsqrt(jnp.mean(o * o, axis=-1, keepdims=True) + LN_EPS) * g_ref[...]
    o_ref[...] = (o * (1.0 - lam_init)).astype(o_ref.dtype)


def _diff_attention(qkv, bias_tiles, lam_q, lam_k, subln_g, layer, lam_init, bsz, seq, tq, tk):
    m = qkv.shape[0]
    hd = 2 * DIFF_HEAD_DIM
    nq = seq // tq
    assert tq == tk, "the block schedule assumes square score tiles"
    r = tq // tk
    diag, corner = bias_tiles
    c = MAX_DISTANCE
    return pl.pallas_call(
        functools.partial(_diff_attn_kernel, lam_init=lam_init, tq=tq, tk=tk),
        grid=(bsz, DIFF_HEADS, nq),
        in_specs=[
            pl.BlockSpec((tq, hd), lambda b, h, i: (b * nq + i, h)),
            pl.BlockSpec((tq, hd), lambda b, h, i: (b * nq + jnp.minimum(i + 1, nq - 1), h)),
            pl.BlockSpec((seq, hd), lambda b, h, i: (b, DIFF_HEADS + h)),
            pl.BlockSpec((seq, hd), lambda b, h, i: (b, 2 * DIFF_HEADS + h)),
            pl.BlockSpec((None, 2, r, tk, tq), lambda b, h, i: (h, 0, 0, 0, 0)),
            pl.BlockSpec((None, 2, c, c), lambda b, h, i: (h, 0, 0, 0)),
            pl.BlockSpec((None, 2, DIFF_HEAD_DIM), lambda b, h, i: (layer, 0, 0)),
            pl.BlockSpec((None, 2, DIFF_HEAD_DIM), lambda b, h, i: (layer, 0, 0)),
            pl.BlockSpec((None, 1, hd), lambda b, h, i: (layer, 0, 0)),
        ],
        out_specs=pl.BlockSpec((tq, hd), lambda b, h, i: (b * nq + i, h)),
        out_shape=jax.ShapeDtypeStruct((m, DIFF_WIDTH), BF16),
        scratch_shapes=[pltpu.VMEM((2, hd, tq), F32),
                        pltpu.VMEM((2, 1, tq), F32),
                        pltpu.VMEM((2, V7X_SUBLANES, tq), F32),
                        pltpu.VMEM((2, 2, tk, tq), F32),
                        pltpu.VMEM((2, 2, V7X_SUBLANES, tq), F32),
                        pltpu.VMEM((2, DIFF_HEAD_DIM, tq), BF16)],
        compiler_params=_params("parallel", "parallel", "arbitrary"),
        name="diff_attention",
    )(qkv, qkv, qkv, qkv, diag, corner, lam_q, lam_k, subln_g.reshape(-1, 1, hd))


def _mem_attn_kernel(q_ref, kv_ref, o_ref):
    d = MEM_HEAD_DIM
    scale = d ** -0.5
    for h in range(MEM_HEADS):
        q = q_ref[:, h * d:(h + 1) * d]
        k = kv_ref[:, h * d:(h + 1) * d]
        v = kv_ref[:, D_MODEL + h * d:D_MODEL + (h + 1) * d]
        s = lax.dot_general(q, k, (((1,), (1,)), ((), ())), preferred_element_type=F32) * scale
        s = s - jnp.max(s, axis=-1, keepdims=True)
        p = jnp.exp(s)
        p = p / jnp.sum(p, axis=-1, keepdims=True)
        o_ref[:, h * d:(h + 1) * d] = jnp.dot(p.astype(BF16), v,
                                              preferred_element_type=F32).astype(o_ref.dtype)


def _mem_attention(q, kv, seq, tm):
    m = q.shape[0]
    tiles_per_seq = seq // tm
    return pl.pallas_call(
        _mem_attn_kernel,
        grid=(m // tm,),
        in_specs=[pl.BlockSpec((tm, D_MODEL), lambda i: (i, 0)),
                  pl.BlockSpec((N_MEM, 2 * D_MODEL), lambda i: (i // tiles_per_seq, 0))],
        out_specs=pl.BlockSpec((tm, D_MODEL), lambda i: (i, 0)),
        out_shape=jax.ShapeDtypeStruct((m, D_MODEL), BF16),
        compiler_params=_params("parallel"),
        name="mem_attention",
    )(q, kv)


def _ffn_kernel(xb_ref, wg_ref, wu_ref, wd_ref, res_ref, g_ref, b_ref, o_ref, ob_ref, acc_ref):
    f = pl.program_id(1)

    def slab(first):
        xb = xb_ref[...]
        gate = jnp.dot(xb, wg_ref[...], preferred_element_type=F32)
        up = jnp.dot(xb, wu_ref[...], preferred_element_type=F32)
        hid = (gate * (1.0 / (1.0 + jnp.exp(-gate))) * up).astype(BF16)
        part = jnp.dot(hid, wd_ref[...], preferred_element_type=F32)
        acc_ref[...] = part if first else acc_ref[...] + part

    @pl.when(f == 0)
    def _():
        slab(True)

    @pl.when(f > 0)
    def _():
        slab(False)

    @pl.when(f == pl.num_programs(1) - 1)
    def _():
        y = _layer_norm_rows(ALPHA * res_ref[...] + acc_ref[...], g_ref[...], b_ref[...])
        o_ref[...] = y
        ob_ref[...] = y.astype(BF16)


def _ffn(xb, w_gu, w_down, layer, res, g, b, tm, tf):
    m, d = xb.shape
    nf = D_FF // tf
    return pl.pallas_call(
        _ffn_kernel,
        grid=(m // tm, nf),
        in_specs=[pl.BlockSpec((tm, d), lambda i, f: (i, 0)),
                  pl.BlockSpec((None, d, tf), lambda i, f: (layer, 0, f)),
                  pl.BlockSpec((None, d, tf), lambda i, f: (layer, 0, nf + f)),
                  pl.BlockSpec((None, tf, d), lambda i, f: (layer, f, 0)),
                  pl.BlockSpec((tm, d), lambda i, f: (i, 0)),
                  pl.BlockSpec((None, 1, d), lambda i, f: (layer, 0, 0)),
                  pl.BlockSpec((None, 1, d), lambda i, f: (layer, 0, 0))],
        out_specs=[pl.BlockSpec((tm, d), lambda i, f: (i, 0)),
                   pl.BlockSpec((tm, d), lambda i, f: (i, 0))],
        out_shape=[jax.ShapeDtypeStruct((m, d), F32),
                   jax.ShapeDtypeStruct((m, d), BF16)],
        scratch_shapes=[pltpu.VMEM((tm, d), F32)],
        compiler_params=_params("parallel", "arbitrary"),
        name="ffn",
    )(xb, w_gu, w_gu, w_down, res, g.reshape(-1, 1, d), b.reshape(-1, 1, d))


class _Tiles:
    proj_rows = 2048
    proj_cols = 512
    mem_q_cols = 1024
    mem_kv_cols = 1024
    res_ln_rows = 512
    mixer_rows = 1024
    mem_attn_rows = 2048
    ffn_rows = 512
    ffn_slab = 512
    attn = 512


def kernel(x, mem, rel_bias, w_in, conv_w, gmlp_ln_g, gmlp_ln_b, gmlp_ws, gmlp_bs,
           diff_lambda_q, diff_lambda_k, diff_subln_g, w_mix_out, ln_mix_g, ln_mix_b,
           w_mem_q, w_mem_kv, w_mem_out, ln_mem_g, ln_mem_b,
           w_ffn_gu, w_ffn_down, ln_ffn_g, ln_ffn_b):
    bsz, seq, d = x.shape
    m = bsz * seq
    x = x.reshape(m, d)
    xb = x.astype(BF16)
    memb = mem.reshape(bsz * N_MEM, d).astype(BF16)
    t = _Tiles
    bias_tiles = _near_bias_tiles(rel_bias, t.attn, t.attn)
    w_in, w_mix_out, w_mem_q, w_mem_kv, w_mem_out, w_ffn_gu, w_ffn_down = (
        w.astype(BF16) for w in (w_in, w_mix_out, w_mem_q, w_mem_kv, w_mem_out,
                                 w_ffn_gu, w_ffn_down))
    gmlp_bs_t = jnp.transpose(gmlp_bs, (0, 2, 1))
    for l in range(DEPTH):
        lam_init = 0.8 - 0.6 * math.exp(-0.3 * l)
        proj_ab = _matmul(xb, w_in, l, 0, AB_COLS, F32, t.proj_rows, t.proj_cols)
        qkv = _matmul(xb, w_in, l, AB_COLS, QKV_COLS, BF16, t.proj_rows, t.proj_cols,
                      scaled_cols=DIFF_WIDTH, scale=ATTN_LOGIT_SCALE)
        y_ab = _mixer_ab(proj_ab, conv_w, gmlp_ln_g, gmlp_ln_b, gmlp_ws, gmlp_bs_t, l, seq,
                         t.mixer_rows)
        y_c = _diff_attention(qkv, bias_tiles, diff_lambda_q, diff_lambda_k, diff_subln_g, l,
                              lam_init, bsz, seq, t.attn, t.attn)
        x, xb = _matmul_res_ln([y_ab, y_c], w_mix_out, l, x, ln_mix_g, ln_mix_b, t.res_ln_rows)

        q_mem = _matmul(xb, w_mem_q, l, 0, D_MODEL, BF16, t.proj_rows, t.mem_q_cols)
        kv_mem = _matmul(memb, w_mem_kv, l, 0, 2 * D_MODEL, BF16, bsz * N_MEM, t.mem_kv_cols)
        o_mem = _mem_attention(q_mem, kv_mem, seq, t.mem_attn_rows)
        x, xb = _matmul_res_ln([o_mem], w_mem_out, l, x, ln_mem_g, ln_mem_b, t.res_ln_rows)

        x, xb = _ffn(xb, w_ffn_gu, w_ffn_down, l, x, ln_ffn_g, ln_ffn_b, t.ffn_rows, t.ffn_slab)
    return x.reshape(bsz, seq, d)
```

```python
import functools
import math

import jax
import jax.numpy as jnp
from jax import lax
from jax.experimental import pallas as pl
from jax.experimental.pallas import tpu as pltpu

D_MODEL = 2048
DEPTH = 2
N_MEM = 256
MEM_HEADS = 4
MEM_HEAD_DIM = D_MODEL // MEM_HEADS
CONV_WIDTH = D_MODEL // 4
CONV_K = 3
GMLP_WIDTH = D_MODEL // 4
GMLP_GROUPS = 4
GMLP_GROUP_DIM = GMLP_WIDTH // GMLP_GROUPS
CHUNK = 128
DIFF_HEAD_DIM = 128
DIFF_WIDTH = D_MODEL // 2
DIFF_HEADS = DIFF_WIDTH // (2 * DIFF_HEAD_DIM)
AB_COLS = 3 * CONV_WIDTH + 2 * GMLP_WIDTH
QKV_COLS = 3 * DIFF_WIDTH
NUM_BUCKETS = 32
MAX_DISTANCE = 128
D_FF = -(-8 * D_MODEL // (3 * 256)) * 256
ALPHA = (2 * DEPTH) ** 0.25
LN_EPS = 1e-5

V7X_VMEM_BYTES = 64 * 1024 * 1024
V7X_LANES = 128
V7X_SUBLANES = 8
VMEM_LIMIT_BYTES = V7X_VMEM_BYTES - 8 * 1024 * 1024

MASK_VALUE = -1e30
LOG2E = math.log2(math.e)
ATTN_LOGIT_SCALE = DIFF_HEAD_DIM ** -0.5 * LOG2E

F32 = jnp.float32
BF16 = jnp.bfloat16


def _params(*semantics):
    return pltpu.CompilerParams(dimension_semantics=semantics,
                                vmem_limit_bytes=VMEM_LIMIT_BYTES)


def _layer_norm_rows(x, g, b):
    mu = jnp.mean(x, axis=-1, keepdims=True)
    xc = x - mu
    var = jnp.mean(xc * xc, axis=-1, keepdims=True)
    return xc * lax.rsqrt(var + LN_EPS) * g + b


def _mm_kernel(a_ref, w_ref, o_ref, *maybe_ab_ref, scaled_tiles, scale):
    w = w_ref[...]
    if w.dtype != BF16:
        w = w.astype(BF16)
    if maybe_ab_ref:
        ab_ref, = maybe_ab_ref

        @pl.when(pl.program_id(1) == 0)
        def _():
            ab_ref[...] = a_ref[...].astype(BF16)

        a = ab_ref[...]
    else:
        a = a_ref[...]
    acc = jnp.dot(a, w, preferred_element_type=F32)
    if scaled_tiles:
        acc = acc * jnp.where(pl.program_id(1) < scaled_tiles, scale, 1.0)
    o_ref[...] = acc.astype(o_ref.dtype)


def _matmul(a, w, layer, col0, n, out_dtype, tm, tn, scaled_cols=0, scale=1.0):
    m, k = a.shape
    j0 = col0 // tn
    cast_a = a.dtype != BF16
    out_specs = [pl.BlockSpec((tm, tn), lambda i, j: (i, j))]
    out_shape = [jax.ShapeDtypeStruct((m, n), out_dtype)]
    if cast_a:
        out_specs.append(pl.BlockSpec((tm, k), lambda i, j: (i, 0)))
        out_shape.append(jax.ShapeDtypeStruct((m, k), BF16))
    outs = pl.pallas_call(
        functools.partial(_mm_kernel, scaled_tiles=scaled_cols // tn, scale=scale),
        grid=(m // tm, n // tn),
        in_specs=[pl.BlockSpec((tm, k), lambda i, j: (i, 0)),
                  pl.BlockSpec((None, k, tn), lambda i, j: (layer, 0, j0 + j))],
        out_specs=out_specs,
        out_shape=out_shape,
        compiler_params=_params("parallel", "arbitrary"),
        name="matmul",
    )(a, w)
    return outs if cast_a else outs[0]


def _mm_res_ln_kernel(*refs, n_a):
    a_refs = refs[:n_a]
    w_ref, res_ref, g_ref, b_ref, o_ref, ob_ref = refs[n_a:]
    a = jnp.concatenate([a_ref[...] for a_ref in a_refs], axis=1)
    acc = jnp.dot(a, w_ref[...], preferred_element_type=F32)
    y = _layer_norm_rows(ALPHA * res_ref[...] + acc, g_ref[...], b_ref[...])
    o_ref[...] = y
    ob_ref[...] = y.astype(BF16)


def _matmul_res_ln(a_list, w, layer, res, g, b, tm):
    m = res.shape[0]
    _, k, n = w.shape
    in_specs = [pl.BlockSpec((tm, a.shape[1]), lambda i: (i, 0)) for a in a_list]
    in_specs += [pl.BlockSpec((None, k, n), lambda i: (layer, 0, 0)),
                 pl.BlockSpec((tm, n), lambda i: (i, 0)),
                 pl.BlockSpec((None, 1, n), lambda i: (layer, 0, 0)),
                 pl.BlockSpec((None, 1, n), lambda i: (layer, 0, 0))]
    return pl.pallas_call(
        functools.partial(_mm_res_ln_kernel, n_a=len(a_list)),
        grid=(m // tm,),
        in_specs=in_specs,
        out_specs=[pl.BlockSpec((tm, n), lambda i: (i, 0)),
                   pl.BlockSpec((tm, n), lambda i: (i, 0))],
        out_shape=[jax.ShapeDtypeStruct((m, n), F32),
                   jax.ShapeDtypeStruct((m, n), BF16)],
        compiler_params=_params("parallel"),
        name="matmul_res_ln",
    )(*a_list, w, res, g.reshape(-1, 1, n), b.reshape(-1, 1, n))


def _mixer_ab_kernel(bg_ref, cg_ref, h_ref, u_ref, v_ref, cgp_ref, hp_ref,
                     cw_ref, lg_ref, lb_ref, ws_ref, bst_ref, o_ref, *, tm, tiles_per_seq):
    i = pl.program_id(0)
    z = cg_ref[...] * h_ref[...]
    zp = cgp_ref[...] * hp_ref[...]
    zp = jnp.where(i % tiles_per_seq == 0, 0.0, zp)
    rows = lax.broadcasted_iota(jnp.int32, (tm, 1), 0)
    last = zp[V7X_SUBLANES - 1:V7X_SUBLANES, :]
    last2 = zp[V7X_SUBLANES - 2:V7X_SUBLANES - 1, :]
    z1 = jnp.where(rows == 0, last, pltpu.roll(z, 1, axis=0))
    z2 = jnp.where(rows == 0, last2, jnp.where(rows == 1, last, pltpu.roll(z, 2, axis=0)))
    cw = cw_ref[...]
    conv = cw[0:1, :] * z2 + cw[1:2, :] * z1 + cw[2:3, :] * z
    o_ref[:, 0:CONV_WIDTH] = (bg_ref[...] * conv).astype(o_ref.dtype)

    u = jax.nn.gelu(u_ref[...])
    vn = _layer_norm_rows(jax.nn.gelu(v_ref[...]), lg_ref[...], lb_ref[...]).astype(BF16)
    tr = lax.broadcasted_iota(jnp.int32, (CHUNK, CHUNK), 0)
    tc = lax.broadcasted_iota(jnp.int32, (CHUNK, CHUNK), 1)
    causal = tr >= tc
    bst = bst_ref[...]
    for g in range(GMLP_GROUPS):
        wg = jnp.where(causal, ws_ref[g], 0.0).astype(BF16)
        bias = bst[:, g:g + 1]
        c0 = g * GMLP_GROUP_DIM
        for c in range(tm // CHUNK):
            r0 = c * CHUNK
            sv = jnp.dot(wg, vn[r0:r0 + CHUNK, c0:c0 + GMLP_GROUP_DIM],
                         preferred_element_type=F32) + bias
            o_ref[r0:r0 + CHUNK, CONV_WIDTH + c0:CONV_WIDTH + c0 + GMLP_GROUP_DIM] = (
                u[r0:r0 + CHUNK, c0:c0 + GMLP_GROUP_DIM] * sv).astype(o_ref.dtype)


def _mixer_ab(proj_ab, conv_w, ln_g, ln_b, ws, bs_t, layer, seq, tm):
    m = proj_ab.shape[0]
    w = CONV_WIDTH
    halo_blocks = tm // V7X_SUBLANES

    def col(c):
        return pl.BlockSpec((tm, w), lambda i, c=c: (i, c))

    def halo(c):
        return pl.BlockSpec((V7X_SUBLANES, w),
                            lambda i, c=c: (jnp.maximum(i * halo_blocks - 1, 0), c))

    def whole(shape):
        return pl.BlockSpec((None,) + shape, lambda i: (layer,) + (0,) * len(shape))

    return pl.pallas_call(
        functools.partial(_mixer_ab_kernel, tm=tm, tiles_per_seq=seq // tm),
        grid=(m // tm,),
        in_specs=[col(0), col(1), col(2), col(3), col(4), halo(1), halo(2),
                  whole((CONV_K, w)), whole((1, w)), whole((1, w)),
                  whole((GMLP_GROUPS, CHUNK, CHUNK)), whole((CHUNK, GMLP_GROUPS))],
        out_specs=pl.BlockSpec((tm, 2 * w), lambda i: (i, 0)),
        out_shape=jax.ShapeDtypeStruct((m, 2 * w), BF16),
        compiler_params=_params("parallel"),
        name="mixer_ab",
    )(proj_ab, proj_ab, proj_ab, proj_ab, proj_ab, proj_ab, proj_ab,
      conv_w, ln_g.reshape(-1, 1, w), ln_b.reshape(-1, 1, w), ws, bs_t)


def _causal_bucket(n):
    max_exact = NUM_BUCKETS // 2
    nf = jnp.maximum(n, 1).astype(F32)
    large = max_exact + (jnp.log(nf / max_exact) / math.log(MAX_DISTANCE / max_exact)
                         * (NUM_BUCKETS - max_exact)).astype(jnp.int32)
    large = jnp.minimum(large, NUM_BUCKETS - 1)
    return jnp.where(n < max_exact, n, large)


def _near_bias_tiles(rel_bias, tq, tk):
    r = tq // tk
    rb = rel_bias.astype(F32)
    rb = jnp.transpose((rb - rb[NUM_BUCKETS - 1]) * LOG2E).reshape(DIFF_HEADS, 2, NUM_BUCKETS)

    def lookup(dist):
        bucket = _causal_bucket(jnp.maximum(dist, 0))
        tiles = jnp.zeros((DIFF_HEADS, 2) + dist.shape, F32)
        for b in range(NUM_BUCKETS - 1):
            tiles = jnp.where(bucket == b, rb[:, :, b].reshape((DIFF_HEADS, 2) + (1,) * dist.ndim),
                              tiles)
        return jnp.where(dist >= 0, tiles, MASK_VALUE)

    qpos = jnp.arange(tq)[None, :]
    kpos = jnp.arange(tk)[:, None]
    diag = lookup(jnp.stack([qpos - kpos - e * tk for e in range(r)], axis=0))
    c = MAX_DISTANCE
    corner = lookup(jnp.arange(c)[None, :] - jnp.arange(c)[:, None] + c)
    return diag, corner


def _diff_attn_kernel(q_ref, qn_ref, k_ref, v_ref, bias_ref, corner_ref, lq_ref, lk_ref, g_ref, o_ref,
                      acc_ref, m_ref, l_ref, t_ref, pm_ref, qt_ref, *, lam_init, tq, tk):
    qi = pl.program_id(2)
    dh = DIFF_HEAD_DIM
    sub = V7X_SUBLANES
    acc_ref[...] = jnp.zeros_like(acc_ref)
    m_ref[...] = jnp.full_like(m_ref, MASK_VALUE)
    l_ref[...] = jnp.zeros_like(l_ref)

    def fold_keys(x, op):
        out = x[0:sub]
        for i in range(1, x.shape[0] // sub):
            out = op(out, x[i * sub:(i + 1) * sub])
        return out

    def scores(j, slot, q_index=qi):
        start = pl.multiple_of(j * tk, tk)
        is_below = (j == q_index - 1).astype(F32)
        c = MAX_DISTANCE
        for mp in range(2):
            k = k_ref[pl.ds(start, tk), mp * dh:(mp + 1) * dh]
            t = jnp.dot(k, qt_ref[mp], preferred_element_type=F32)
            head, tail = t[0:tk - c], t[tk - c:tk]
            tail = jnp.concatenate([tail[:, 0:c] + is_below * corner_ref[mp], tail[:, c:]], axis=1)
            t_ref[slot, mp, 0:tk - c, :] = head
            t_ref[slot, mp, tk - c:tk, :] = tail
            pm_ref[slot, mp] = jnp.maximum(fold_keys(head, jnp.maximum),
                                           fold_keys(tail, jnp.maximum))

    def accumulate(j, slot, bias_tile):
        start = pl.multiple_of(j * tk, tk)
        v_t = jnp.transpose(v_ref[pl.ds(start, tk), :])
        for mp in range(2):
            t = t_ref[slot, mp]
            if bias_tile is None:
                pm = pm_ref[slot, mp]
            else:
                t = t + bias_ref[mp, bias_tile]
                pm = fold_keys(t, jnp.maximum)
            m_prev = m_ref[mp]
            m_new = jnp.maximum(m_prev, jnp.max(pm, axis=0, keepdims=True))
            p = jnp.exp2(t - m_new)
            a = jnp.exp2(m_prev - m_new)
            l_ref[mp] = a * l_ref[mp] + fold_keys(p, jnp.add)
            m_ref[mp] = m_new
            pv = jnp.dot(v_t, p.astype(BF16), preferred_element_type=F32)
            acc_ref[mp] = a * acc_ref[mp] + pv

    def pipelined(j, count):
        for n in range(count):
            scores(j + n + 1, (n + 1) % 2)
            accumulate(j + n, n % 2, None)

    def prepare(src_ref, q_index):
        for mp in range(2):
            q = src_ref[:, mp * dh:(mp + 1) * dh].astype(F32)
            qt_ref[mp] = jnp.transpose(q).astype(BF16)
        scores(0, 0, q_index)

    @pl.when(qi == 0)
    def _():
        prepare(q_ref, qi)

    quads = qi // 4
    rem = qi % 4

    @pl.loop(0, quads)
    def _(i):
        pipelined(4 * i, 4)

    @pl.when(rem >= 2)
    def _():
        pipelined(4 * quads, 2)

    @pl.when(rem % 2 == 0)
    def _():
        accumulate(qi, 0, 0)

    @pl.when(rem % 2 == 1)
    def _():
        scores(qi, 1)
        accumulate(qi - 1, 0, None)
        accumulate(qi, 1, 0)

    prepare(qn_ref, qi + 1)
    outs = []
    for mp in range(2):
        l = jnp.sum(l_ref[mp], axis=0, keepdims=True)
        outs.append(acc_ref[mp] / l)
    sums = jnp.sum(lq_ref[...] * lk_ref[...], axis=-1, keepdims=True)
    ex = jnp.exp(sums)
    lam = ex[0:1, :] - ex[1:2, :] + lam_init
    o = jnp.transpose(outs[0] - lam * outs[1])
    o = o * lax.rsqrt(jnp.mean(o * o, axis=-1, keepdims=True) + LN_EPS) * g_ref[...]
    o_ref[...] = (o * (1.0 - lam_init)).astype(o_ref.dtype)


def _diff_attention(qkv, bias_tiles, lam_q, lam_k, subln_g, layer, lam_init, bsz, seq, tq, tk):
    m = qkv.shape[0]
    hd = 2 * DIFF_HEAD_DIM
    nq = seq // tq
    assert tq == tk, "the block schedule assumes square score tiles"
    r = tq // tk
    diag, corner = bias_tiles
    c = MAX_DISTANCE
    return pl.pallas_call(
        functools.partial(_diff_attn_kernel, lam_init=lam_init, tq=tq, tk=tk),
        grid=(bsz, DIFF_HEADS, nq),
        in_specs=[
            pl.BlockSpec((tq, hd), lambda b, h, i: (b * nq + i, h)),
            pl.BlockSpec((tq, hd), lambda b, h, i: (b * nq + jnp.minimum(i + 1, nq - 1), h)),
            pl.BlockSpec((seq, hd), lambda b, h, i: (b, DIFF_HEADS + h)),
            pl.BlockSpec((seq, hd), lambda b, h, i: (b, 2 * DIFF_HEADS + h)),
            pl.BlockSpec((None, 2, r, tk, tq), lambda b, h, i: (h, 0, 0, 0, 0)),
            pl.BlockSpec((None, 2, c, c), lambda b, h, i: (h, 0, 0, 0)),
            pl.BlockSpec((None, 2, DIFF_HEAD_DIM), lambda b, h, i: (layer, 0, 0)),
            pl.BlockSpec((None, 2, DIFF_HEAD_DIM), lambda b, h, i: (layer, 0, 0)),
            pl.BlockSpec((None, 1, hd), lambda b, h, i: (layer, 0, 0)),
        ],
        out_specs=pl.BlockSpec((tq, hd), lambda b, h, i: (b * nq + i, h)),
        out_shape=jax.ShapeDtypeStruct((m, DIFF_WIDTH), BF16),
        scratch_shapes=[pltpu.VMEM((2, hd, tq), F32),
                        pltpu.VMEM((2, 1, tq), F32),
                        pltpu.VMEM((2, V7X_SUBLANES, tq), F32),
                        pltpu.VMEM((2, 2, tk, tq), F32),
                        pltpu.VMEM((2, 2, V7X_SUBLANES, tq), F32),
                        pltpu.VMEM((2, DIFF_HEAD_DIM, tq), BF16)],
        compiler_params=_params("parallel", "parallel", "arbitrary"),
        name="diff_attention",
    )(qkv, qkv, qkv, qkv, diag, corner, lam_q, lam_k, subln_g.reshape(-1, 1, hd))


def _mem_attn_kernel(q_ref, kv_ref, o_ref):
    d = MEM_HEAD_DIM
    scale = d ** -0.5
    for h in range(MEM_HEADS):
        q = q_ref[:, h * d:(h + 1) * d]
        k = kv_ref[:, h * d:(h + 1) * d]
        v = kv_ref[:, D_MODEL + h * d:D_MODEL + (h + 1) * d]
        s = lax.dot_general(q, k, (((1,), (1,)), ((), ())), preferred_element_type=F32) * scale
        s = s - jnp.max(s, axis=-1, keepdims=True)
        p = jnp.exp(s)
        p = p / jnp.sum(p, axis=-1, keepdims=True)
        o_ref[:, h * d:(h + 1) * d] = jnp.dot(p.astype(BF16), v,
                                              preferred_element_type=F32).astype(o_ref.dtype)


def _mem_attention(q, kv, seq, tm):
    m = q.shape[0]
    tiles_per_seq = seq // tm
    return pl.pallas_call(
        _mem_attn_kernel,
        grid=(m // tm,),
        in_specs=[pl.BlockSpec((tm, D_MODEL), lambda i: (i, 0)),
                  pl.BlockSpec((N_MEM, 2 * D_MODEL), lambda i: (i // tiles_per_seq, 0))],
        out_specs=pl.BlockSpec((tm, D_MODEL), lambda i: (i, 0)),
        out_shape=jax.ShapeDtypeStruct((m, D_MODEL), BF16),
        compiler_params=_params("parallel"),
        name="mem_attention",
    )(q, kv)


def _ffn_kernel(xb_ref, wg_ref, wu_ref, wd_ref, res_ref, g_ref, b_ref, o_ref, ob_ref, acc_ref):
    f = pl.program_id(1)

    def slab(first):
        xb = xb_ref[...]
        gate = jnp.dot(xb, wg_ref[...], preferred_element_type=F32)
        up = jnp.dot(xb, wu_ref[...], preferred_element_type=F32)
        hid = (gate * (1.0 / (1.0 + jnp.exp(-gate))) * up).astype(BF16)
        part = jnp.dot(hid, wd_ref[...], preferred_element_type=F32)
        acc_ref[...] = part if first else acc_ref[...] + part

    @pl.when(f == 0)
    def _():
        slab(True)

    @pl.when(f > 0)
    def _():
        slab(False)

    @pl.when(f == pl.num_programs(1) - 1)
    def _():
        y = _layer_norm_rows(ALPHA * res_ref[...] + acc_ref[...], g_ref[...], b_ref[...])
        o_ref[...] = y
        ob_ref[...] = y.astype(BF16)


def _ffn(xb, w_gu, w_down, layer, res, g, b, tm, tf):
    m, d = xb.shape
    nf = D_FF // tf
    return pl.pallas_call(
        _ffn_kernel,
        grid=(m // tm, nf),
        in_specs=[pl.BlockSpec((tm, d), lambda i, f: (i, 0)),
                  pl.BlockSpec((None, d, tf), lambda i, f: (layer, 0, f)),
                  pl.BlockSpec((None, d, tf), lambda i, f: (layer, 0, nf + f)),
                  pl.BlockSpec((None, tf, d), lambda i, f: (layer, f, 0)),
                  pl.BlockSpec((tm, d), lambda i, f: (i, 0)),
                  pl.BlockSpec((None, 1, d), lambda i, f: (layer, 0, 0)),
                  pl.BlockSpec((None, 1, d), lambda i, f: (layer, 0, 0))],
        out_specs=[pl.BlockSpec((tm, d), lambda i, f: (i, 0)),
                   pl.BlockSpec((tm, d), lambda i, f: (i, 0))],
        out_shape=[jax.ShapeDtypeStruct((m, d), F32),
                   jax.ShapeDtypeStruct((m, d), BF16)],
        scratch_shapes=[pltpu.VMEM((tm, d), F32)],
        compiler_params=_params("parallel", "arbitrary"),
        name="ffn",
    )(xb, w_gu, w_gu, w_down, res, g.reshape(-1, 1, d), b.reshape(-1, 1, d))


class _Tiles:
    proj_rows = 2048
    proj_rows_f32 = 1024
    proj_cols = 512
    mem_q_cols = 1024
    mem_kv_cols = 1024
    res_ln_rows = 512
    mixer_rows = 1024
    mem_attn_rows = 2048
    ffn_rows = 512
    ffn_slab = 512
    attn = 512


def kernel(x, mem, rel_bias, w_in, conv_w, gmlp_ln_g, gmlp_ln_b, gmlp_ws, gmlp_bs,
           diff_lambda_q, diff_lambda_k, diff_subln_g, w_mix_out, ln_mix_g, ln_mix_b,
           w_mem_q, w_mem_kv, w_mem_out, ln_mem_g, ln_mem_b,
           w_ffn_gu, w_ffn_down, ln_ffn_g, ln_ffn_b):
    bsz, seq, d = x.shape
    m = bsz * seq
    x = x.reshape(m, d)
    xb = None
    memb = mem.reshape(bsz * N_MEM, d).astype(BF16)
    t = _Tiles
    bias_tiles = _near_bias_tiles(rel_bias, t.attn, t.attn)
    w_in, w_mix_out, w_mem_q, w_mem_out, w_ffn_gu, w_ffn_down = (
        w.astype(BF16) for w in (w_in, w_mix_out, w_mem_q, w_mem_out, w_ffn_gu, w_ffn_down))
    gmlp_bs_t = jnp.transpose(gmlp_bs, (0, 2, 1))
    for l in range(DEPTH):
        lam_init = 0.8 - 0.6 * math.exp(-0.3 * l)
        if xb is None:
            proj_ab, xb = _matmul(x, w_in, l, 0, AB_COLS, F32, t.proj_rows_f32, t.proj_cols)
        else:
            proj_ab = _matmul(xb, w_in, l, 0, AB_COLS, F32, t.proj_rows, t.proj_cols)
        qkv = _matmul(xb, w_in, l, AB_COLS, QKV_COLS, BF16, t.proj_rows, t.proj_cols,
                      scaled_cols=DIFF_WIDTH, scale=ATTN_LOGIT_SCALE)
        y_ab = _mixer_ab(proj_ab, conv_w, gmlp_ln_g, gmlp_ln_b, gmlp_ws, gmlp_bs_t, l, seq,
                         t.mixer_rows)
        y_c = _diff_attention(qkv, bias_tiles, diff_lambda_q, diff_lambda_k, diff_subln_g, l,
                              lam_init, bsz, seq, t.attn, t.attn)
        x, xb = _matmul_res_ln([y_ab, y_c], w_mix_out, l, x, ln_mix_g, ln_mix_b, t.res_ln_rows)

        q_mem = _matmul(xb, w_mem_q, l, 0, D_MODEL, BF16, t.proj_rows, t.mem_q_cols)
        kv_mem = _matmul(memb, w_mem_kv, l, 0, 2 * D_MODEL, BF16, bsz * N_MEM, t.mem_kv_cols)
        o_mem = _mem_attention(q_mem, kv_mem, seq, t.mem_attn_rows)
        x, xb = _matmul_res_ln([o_mem], w_mem_out, l, x, ln_mem_g, ln_mem_b, t.res_ln_rows)

        x, xb = _ffn(xb, w_ffn_gu, w_ffn_down, l, x, ln_ffn_g, ln_ffn_b, t.ffn_rows, t.ffn_slab)
    return x.reshape(bsz, seq, d)
```

```python
import functools
import math

import jax
import jax.numpy as jnp
from jax import lax
from jax.experimental import pallas as pl
from jax.experimental.pallas import tpu as pltpu

D_MODEL = 2048
DEPTH = 2
N_MEM = 256
MEM_HEADS = 4
MEM_HEAD_DIM = D_MODEL // MEM_HEADS
CONV_WIDTH = D_MODEL // 4
CONV_K = 3
GMLP_WIDTH = D_MODEL // 4
GMLP_GROUPS = 4
GMLP_GROUP_DIM = GMLP_WIDTH // GMLP_GROUPS
CHUNK = 128
DIFF_HEAD_DIM = 128
DIFF_WIDTH = D_MODEL // 2
DIFF_HEADS = DIFF_WIDTH // (2 * DIFF_HEAD_DIM)
AB_COLS = 3 * CONV_WIDTH + 2 * GMLP_WIDTH
QKV_COLS = 3 * DIFF_WIDTH
NUM_BUCKETS = 32
MAX_DISTANCE = 128
D_FF = -(-8 * D_MODEL // (3 * 256)) * 256
ALPHA = (2 * DEPTH) ** 0.25
LN_EPS = 1e-5

V7X_VMEM_BYTES = 64 * 1024 * 1024
V7X_LANES = 128
V7X_SUBLANES = 8
VMEM_LIMIT_BYTES = V7X_VMEM_BYTES - 8 * 1024 * 1024

MASK_VALUE = -1e30
LOG2E = math.log2(math.e)
ATTN_LOGIT_SCALE = DIFF_HEAD_DIM ** -0.5 * LOG2E

F32 = jnp.float32
BF16 = jnp.bfloat16


def _params(*semantics):
    return pltpu.CompilerParams(dimension_semantics=semantics,
                                vmem_limit_bytes=VMEM_LIMIT_BYTES)


def _layer_norm_rows(x, g, b):
    mu = jnp.mean(x, axis=-1, keepdims=True)
    xc = x - mu
    var = jnp.mean(xc * xc, axis=-1, keepdims=True)
    return xc * lax.rsqrt(var + LN_EPS) * g + b


def _mm_kernel(a_ref, w_ref, o_ref, *maybe_ab_ref, scaled_tiles, scale):
    w = w_ref[...]
    if w.dtype != BF16:
        w = w.astype(BF16)
    if maybe_ab_ref:
        ab_ref, = maybe_ab_ref

        @pl.when(pl.program_id(1) == 0)
        def _():
            ab_ref[...] = a_ref[...].astype(BF16)

        a = ab_ref[...]
    else:
        a = a_ref[...]
    acc = jnp.dot(a, w, preferred_element_type=F32)
    if scaled_tiles:
        acc = acc * jnp.where(pl.program_id(1) < scaled_tiles, scale, 1.0)
    o_ref[...] = acc.astype(o_ref.dtype)


def _matmul(a, w, layer, col0, n, out_dtype, tm, tn, scaled_cols=0, scale=1.0):
    m, k = a.shape
    j0 = col0 // tn
    cast_a = a.dtype != BF16
    out_specs = [pl.BlockSpec((tm, tn), lambda i, j: (i, j))]
    out_shape = [jax.ShapeDtypeStruct((m, n), out_dtype)]
    if cast_a:
        out_specs.append(pl.BlockSpec((tm, k), lambda i, j: (i, 0)))
        out_shape.append(jax.ShapeDtypeStruct((m, k), BF16))
    outs = pl.pallas_call(
        functools.partial(_mm_kernel, scaled_tiles=scaled_cols // tn, scale=scale),
        grid=(m // tm, n // tn),
        in_specs=[pl.BlockSpec((tm, k), lambda i, j: (i, 0)),
                  pl.BlockSpec((None, k, tn), lambda i, j: (layer, 0, j0 + j))],
        out_specs=out_specs,
        out_shape=out_shape,
        compiler_params=_params("parallel", "arbitrary"),
        name="matmul",
    )(a, w)
    return outs if cast_a else outs[0]


def _mm_res_ln_kernel(*refs, n_a):
    a_refs = refs[:n_a]
    w_ref, res_ref, g_ref, b_ref, o_ref, ob_ref = refs[n_a:]
    a = jnp.concatenate([a_ref[...] for a_ref in a_refs], axis=1)
    acc = jnp.dot(a, w_ref[...], preferred_element_type=F32)
    y = _layer_norm_rows(ALPHA * res_ref[...] + acc, g_ref[...], b_ref[...])
    o_ref[...] = y
    ob_ref[...] = y.astype(BF16)


def _matmul_res_ln(a_list, w, layer, res, g, b, tm):
    m = res.shape[0]
    _, k, n = w.shape
    in_specs = [pl.BlockSpec((tm, a.shape[1]), lambda i: (i, 0)) for a in a_list]
    in_specs += [pl.BlockSpec((None, k, n), lambda i: (layer, 0, 0)),
                 pl.BlockSpec((tm, n), lambda i: (i, 0)),
                 pl.BlockSpec((None, 1, n), lambda i: (layer, 0, 0)),
                 pl.BlockSpec((None, 1, n), lambda i: (layer, 0, 0))]
    return pl.pallas_call(
        functools.partial(_mm_res_ln_kernel, n_a=len(a_list)),
        grid=(m // tm,),
        in_specs=in_specs,
        out_specs=[pl.BlockSpec((tm, n), lambda i: (i, 0)),
                   pl.BlockSpec((tm, n), lambda i: (i, 0))],
        out_shape=[jax.ShapeDtypeStruct((m, n), F32),
                   jax.ShapeDtypeStruct((m, n), BF16)],
        compiler_params=_params("parallel"),
        name="matmul_res_ln",
    )(*a_list, w, res, g.reshape(-1, 1, n), b.reshape(-1, 1, n))


def _mixer_ab_kernel(bg_ref, cg_ref, h_ref, u_ref, v_ref, cgp_ref, hp_ref,
                     cw_ref, lg_ref, lb_ref, ws_ref, bst_ref, o_ref, *, tm, tiles_per_seq):
    i = pl.program_id(0)
    z = cg_ref[...] * h_ref[...]
    zp = cgp_ref[...] * hp_ref[...]
    zp = jnp.where(i % tiles_per_seq == 0, 0.0, zp)
    rows = lax.broadcasted_iota(jnp.int32, (tm, 1), 0)
    last = zp[V7X_SUBLANES - 1:V7X_SUBLANES, :]
    last2 = zp[V7X_SUBLANES - 2:V7X_SUBLANES - 1, :]
    z1 = jnp.where(rows == 0, last, pltpu.roll(z, 1, axis=0))
    z2 = jnp.where(rows == 0, last2, jnp.where(rows == 1, last, pltpu.roll(z, 2, axis=0)))
    cw = cw_ref[...]
    conv = cw[0:1, :] * z2 + cw[1:2, :] * z1 + cw[2:3, :] * z
    o_ref[:, 0:CONV_WIDTH] = (bg_ref[...] * conv).astype(o_ref.dtype)

    u = jax.nn.gelu(u_ref[...])
    vn = _layer_norm_rows(jax.nn.gelu(v_ref[...]), lg_ref[...], lb_ref[...]).astype(BF16)
    tr = lax.broadcasted_iota(jnp.int32, (CHUNK, CHUNK), 0)
    tc = lax.broadcasted_iota(jnp.int32, (CHUNK, CHUNK), 1)
    causal = tr >= tc
    bst = bst_ref[...]
    for g in range(GMLP_GROUPS):
        wg = jnp.where(causal, ws_ref[g], 0.0).astype(BF16)
        bias = bst[:, g:g + 1]
        c0 = g * GMLP_GROUP_DIM
        for c in range(tm // CHUNK):
            r0 = c * CHUNK
            sv = jnp.dot(wg, vn[r0:r0 + CHUNK, c0:c0 + GMLP_GROUP_DIM],
                         preferred_element_type=F32) + bias
            o_ref[r0:r0 + CHUNK, CONV_WIDTH + c0:CONV_WIDTH + c0 + GMLP_GROUP_DIM] = (
                u[r0:r0 + CHUNK, c0:c0 + GMLP_GROUP_DIM] * sv).astype(o_ref.dtype)


def _mixer_ab(proj_ab, conv_w, ln_g, ln_b, ws, bs_t, layer, seq, tm):
    m = proj_ab.shape[0]
    w = CONV_WIDTH
    halo_blocks = tm // V7X_SUBLANES

    def col(c):
        return pl.BlockSpec((tm, w), lambda i, c=c: (i, c))

    def halo(c):
        return pl.BlockSpec((V7X_SUBLANES, w),
                            lambda i, c=c: (jnp.maximum(i * halo_blocks - 1, 0), c))

    def whole(shape):
        return pl.BlockSpec((None,) + shape, lambda i: (layer,) + (0,) * len(shape))

    return pl.pallas_call(
        functools.partial(_mixer_ab_kernel, tm=tm, tiles_per_seq=seq // tm),
        grid=(m // tm,),
        in_specs=[col(0), col(1), col(2), col(3), col(4), halo(1), halo(2),
                  whole((CONV_K, w)), whole((1, w)), whole((1, w)),
                  whole((GMLP_GROUPS, CHUNK, CHUNK)), whole((CHUNK, GMLP_GROUPS))],
        out_specs=pl.BlockSpec((tm, 2 * w), lambda i: (i, 0)),
        out_shape=jax.ShapeDtypeStruct((m, 2 * w), BF16),
        compiler_params=_params("parallel"),
        name="mixer_ab",
    )(proj_ab, proj_ab, proj_ab, proj_ab, proj_ab, proj_ab, proj_ab,
      conv_w, ln_g.reshape(-1, 1, w), ln_b.reshape(-1, 1, w), ws, bs_t)


def _causal_bucket(n):
    max_exact = NUM_BUCKETS // 2
    nf = jnp.maximum(n, 1).astype(F32)
    large = max_exact + (jnp.log(nf / max_exact) / math.log(MAX_DISTANCE / max_exact)
                         * (NUM_BUCKETS - max_exact)).astype(jnp.int32)
    large = jnp.minimum(large, NUM_BUCKETS - 1)
    return jnp.where(n < max_exact, n, large)


def _near_bias_tiles(rel_bias, tq, tk):
    r = tq // tk
    rb = rel_bias.astype(F32)
    rb = jnp.transpose((rb - rb[NUM_BUCKETS - 1]) * LOG2E).reshape(DIFF_HEADS, 2, NUM_BUCKETS)

    def lookup(dist):
        bucket = _causal_bucket(jnp.maximum(dist, 0))
        tiles = jnp.zeros((DIFF_HEADS, 2) + dist.shape, F32)
        for b in range(NUM_BUCKETS - 1):
            tiles = jnp.where(bucket == b, rb[:, :, b].reshape((DIFF_HEADS, 2) + (1,) * dist.ndim),
                              tiles)
        return jnp.where(dist >= 0, tiles, MASK_VALUE)

    qpos = jnp.arange(tq)[None, :]
    kpos = jnp.arange(tk)[:, None]
    diag = lookup(jnp.stack([qpos - kpos - e * tk for e in range(r)], axis=0))
    c = MAX_DISTANCE
    corner = lookup(jnp.arange(c)[None, :] - jnp.arange(c)[:, None] + c)
    return diag, corner


def _diff_attn_kernel(q_ref, qn_ref, k_ref, v_ref, bias_ref, corner_ref, lq_ref, lk_ref, g_ref, o_ref,
                      acc_ref, m_ref, l_ref, t_ref, pm_ref, qt_ref, *, lam_init, tq, tk):
    qi = pl.program_id(2)
    dh = DIFF_HEAD_DIM
    sub = V7X_SUBLANES
    acc_ref[...] = jnp.zeros_like(acc_ref)
    m_ref[...] = jnp.full_like(m_ref, MASK_VALUE)
    l_ref[...] = jnp.zeros_like(l_ref)

    def fold_keys(x, op):
        out = x[0:sub]
        for i in range(1, x.shape[0] // sub):
            out = op(out, x[i * sub:(i + 1) * sub])
        return out

    def scores(j, slot, q_index=qi):
        start = pl.multiple_of(j * tk, tk)
        is_below = (j == q_index - 1).astype(F32)
        c = MAX_DISTANCE
        for mp in range(2):
            k = k_ref[pl.ds(start, tk), mp * dh:(mp + 1) * dh]
            t = jnp.dot(k, qt_ref[mp], preferred_element_type=F32)
            head, tail = t[0:tk - c], t[tk - c:tk]
            tail = jnp.concatenate([tail[:, 0:c] + is_below * corner_ref[mp], tail[:, c:]], axis=1)
            t_ref[slot, mp, 0:tk - c, :] = head
            t_ref[slot, mp, tk - c:tk, :] = tail
            pm_ref[slot, mp] = jnp.maximum(fold_keys(head, jnp.maximum),
                                           fold_keys(tail, jnp.maximum))

    def accumulate(j, slot, bias_tile):
        start = pl.multiple_of(j * tk, tk)
        v_t = jnp.transpose(v_ref[pl.ds(start, tk), :])
        for mp in range(2):
            t = t_ref[slot, mp]
            if bias_tile is None:
                pm = pm_ref[slot, mp]
            else:
                t = t + bias_ref[mp, bias_tile]
                pm = fold_keys(t, jnp.maximum)
            m_prev = m_ref[mp]
            m_new = jnp.maximum(m_prev, jnp.max(pm, axis=0, keepdims=True))
            p = jnp.exp2(t - m_new)
            a = jnp.exp2(m_prev - m_new)
            l_ref[mp] = a * l_ref[mp] + fold_keys(p, jnp.add)
            m_ref[mp] = m_new
            pv = jnp.dot(v_t, p.astype(BF16), preferred_element_type=F32)
            acc_ref[mp] = a * acc_ref[mp] + pv

    def pipelined(j, count):
        for n in range(count):
            scores(j + n + 1, (n + 1) % 2)
            accumulate(j + n, n % 2, None)

    def prepare(src_ref, q_index):
        for mp in range(2):
            q = src_ref[:, mp * dh:(mp + 1) * dh].astype(F32)
            qt_ref[mp] = jnp.transpose(q).astype(BF16)
        scores(0, 0, q_index)

    @pl.when(qi == 0)
    def _():
        prepare(q_ref, qi)

    quads = qi // 4
    rem = qi % 4

    @pl.loop(0, quads)
    def _(i):
        pipelined(4 * i, 4)

    @pl.when(rem >= 2)
    def _():
        pipelined(4 * quads, 2)

    @pl.when(rem % 2 == 0)
    def _():
        accumulate(qi, 0, 0)

    @pl.when(rem % 2 == 1)
    def _():
        scores(qi, 1)
        accumulate(qi - 1, 0, None)
        accumulate(qi, 1, 0)

    prepare(qn_ref, qi + 1)
    outs = []
    for mp in range(2):
        l = jnp.sum(l_ref[mp], axis=0, keepdims=True)
        outs.append(acc_ref[mp] / l)
    sums = jnp.sum(lq_ref[...] * lk_ref[...], axis=-1, keepdims=True)
    ex = jnp.exp(sums)
    lam = ex[0:1, :] - ex[1:2, :] + lam_init
    o = jnp.transpose(outs[0] - lam * outs[1])
    o = o * lax.rsqrt(jnp.mean(o * o, axis=-1, keepdims=True) + LN_EPS) * g_ref[...]
    o_ref[...] = (o * (1.0 - lam_init)).astype(o_ref.dtype)


def _diff_attention(qkv, bias_tiles, lam_q, lam_k, subln_g, layer, lam_init, bsz, seq, tq, tk):
    m = qkv.shape[0]
    hd = 2 * DIFF_HEAD_DIM
    nq = seq // tq
    assert tq == tk, "the block schedule assumes square score tiles"
    r = tq // tk
    diag, corner = bias_tiles
    c = MAX_DISTANCE
    return pl.pallas_call(
        functools.partial(_diff_attn_kernel, lam_init=lam_init, tq=tq, tk=tk),
        grid=(bsz, DIFF_HEADS, nq),
        in_specs=[
            pl.BlockSpec((tq, hd), lambda b, h, i: (b * nq + i, h)),
            pl.BlockSpec((tq, hd), lambda b, h, i: (b * nq + jnp.minimum(i + 1, nq - 1), h)),
            pl.BlockSpec((seq, hd), lambda b, h, i: (b, DIFF_HEADS + h)),
            pl.BlockSpec((seq, hd), lambda b, h, i: (b, 2 * DIFF_HEADS + h)),
            pl.BlockSpec((None, 2, r, tk, tq), lambda b, h, i: (h, 0, 0, 0, 0)),
            pl.BlockSpec((None, 2, c, c), lambda b, h, i: (h, 0, 0, 0)),
            pl.BlockSpec((None, 2, DIFF_HEAD_DIM), lambda b, h, i: (layer, 0, 0)),
            pl.BlockSpec((None, 2, DIFF_HEAD_DIM), lambda b, h, i: (layer, 0, 0)),
            pl.BlockSpec((None, 1, hd), lambda b, h, i: (layer, 0, 0)),
        ],
        out_specs=pl.BlockSpec((tq, hd), lambda b, h, i: (b * nq + i, h)),
        out_shape=jax.ShapeDtypeStruct((m, DIFF_WIDTH), BF16),
        scratch_shapes=[pltpu.VMEM((2, hd, tq), F32),
                        pltpu.VMEM((2, 1, tq), F32),
                        pltpu.VMEM((2, V7X_SUBLANES, tq), F32),
                        pltpu.VMEM((2, 2, tk, tq), F32),
                        pltpu.VMEM((2, 2, V7X_SUBLANES, tq), F32),
                        pltpu.VMEM((2, DIFF_HEAD_DIM, tq), BF16)],
        compiler_params=_params("parallel", "parallel", "arbitrary"),
        name="diff_attention",
    )(qkv, qkv, qkv, qkv, diag, corner, lam_q, lam_k, subln_g.reshape(-1, 1, hd))


def _mem_attn_kernel(q_ref, kv_ref, o_ref):
    d = MEM_HEAD_DIM
    scale = d ** -0.5
    for h in range(MEM_HEADS):
        q = q_ref[:, h * d:(h + 1) * d]
        k = kv_ref[:, h * d:(h + 1) * d]
        v = kv_ref[:, D_MODEL + h * d:D_MODEL + (h + 1) * d]
        s = lax.dot_general(q, k, (((1,), (1,)), ((), ())), preferred_element_type=F32) * scale
        s = s - jnp.max(s, axis=-1, keepdims=True)
        p = jnp.exp(s)
        p = p / jnp.sum(p, axis=-1, keepdims=True)
        o_ref[:, h * d:(h + 1) * d] = jnp.dot(p.astype(BF16), v,
                                              preferred_element_type=F32).astype(o_ref.dtype)


def _mem_attention(q, kv, seq, tm):
    m = q.shape[0]
    tiles_per_seq = seq // tm
    return pl.pallas_call(
        _mem_attn_kernel,
        grid=(m // tm,),
        in_specs=[pl.BlockSpec((tm, D_MODEL), lambda i: (i, 0)),
                  pl.BlockSpec((N_MEM, 2 * D_MODEL), lambda i: (i // tiles_per_seq, 0))],
        out_specs=pl.BlockSpec((tm, D_MODEL), lambda i: (i, 0)),
        out_shape=jax.ShapeDtypeStruct((m, D_MODEL), BF16),
        compiler_params=_params("parallel"),
        name="mem_attention",
    )(q, kv)


def _ffn_kernel(xb_ref, wg_ref, wu_ref, wd_ref, res_ref, g_ref, b_ref, o_ref, ob_ref, acc_ref):
    f = pl.program_id(1)

    def slab(first):
        xb = xb_ref[...]
        gate = jnp.dot(xb, wg_ref[...], preferred_element_type=F32)
        up = jnp.dot(xb, wu_ref[...], preferred_element_type=F32)
        hid = (gate * (1.0 / (1.0 + jnp.exp(-gate))) * up).astype(BF16)
        part = jnp.dot(hid, wd_ref[...], preferred_element_type=F32)
        acc_ref[...] = part if first else acc_ref[...] + part

    @pl.when(f == 0)
    def _():
        slab(True)

    @pl.when(f > 0)
    def _():
        slab(False)

    @pl.when(f == pl.num_programs(1) - 1)
    def _():
        y = _layer_norm_rows(ALPHA * res_ref[...] + acc_ref[...], g_ref[...], b_ref[...])
        o_ref[...] = y
        ob_ref[...] = y.astype(BF16)


def _ffn(xb, w_gu, w_down, layer, res, g, b, tm, tf):
    m, d = xb.shape
    nf = D_FF // tf
    return pl.pallas_call(
        _ffn_kernel,
        grid=(m // tm, nf),
        in_specs=[pl.BlockSpec((tm, d), lambda i, f: (i, 0)),
                  pl.BlockSpec((None, d, tf), lambda i, f: (layer, 0, f)),
                  pl.BlockSpec((None, d, tf), lambda i, f: (layer, 0, nf + f)),
                  pl.BlockSpec((None, tf, d), lambda i, f: (layer, f, 0)),
                  pl.BlockSpec((tm, d), lambda i, f: (i, 0)),
                  pl.BlockSpec((None, 1, d), lambda i, f: (layer, 0, 0)),
                  pl.BlockSpec((None, 1, d), lambda i, f: (layer, 0, 0))],
        out_specs=[pl.BlockSpec((tm, d), lambda i, f: (i, 0)),
                   pl.BlockSpec((tm, d), lambda i, f: (i, 0))],
        out_shape=[jax.ShapeDtypeStruct((m, d), F32),
                   jax.ShapeDtypeStruct((m, d), BF16)],
        scratch_shapes=[pltpu.VMEM((tm, d), F32)],
        compiler_params=_params("parallel", "arbitrary"),
        name="ffn",
    )(xb, w_gu, w_gu, w_down, res, g.reshape(-1, 1, d), b.reshape(-1, 1, d))


class _Tiles:
    proj_rows = 2048
    proj_rows_f32 = 1024
    proj_cols_f32 = 1280
    proj_cols = 512
    mem_q_cols = 1024
    mem_kv_cols = 1024
    res_ln_rows = 512
    mixer_rows = 1024
    mem_attn_rows = 2048
    ffn_rows = 512
    ffn_slab = 512
    attn = 512


def kernel(x, mem, rel_bias, w_in, conv_w, gmlp_ln_g, gmlp_ln_b, gmlp_ws, gmlp_bs,
           diff_lambda_q, diff_lambda_k, diff_subln_g, w_mix_out, ln_mix_g, ln_mix_b,
           w_mem_q, w_mem_kv, w_mem_out, ln_mem_g, ln_mem_b,
           w_ffn_gu, w_ffn_down, ln_ffn_g, ln_ffn_b):
    bsz, seq, d = x.shape
    m = bsz * seq
    x = x.reshape(m, d)
    xb = None
    memb = mem.reshape(bsz * N_MEM, d).astype(BF16)
    t = _Tiles
    bias_tiles = _near_bias_tiles(rel_bias, t.attn, t.attn)
    w_in, w_mix_out, w_mem_q, w_mem_out, w_ffn_gu, w_ffn_down = (
        w.astype(BF16) for w in (w_in, w_mix_out, w_mem_q, w_mem_out, w_ffn_gu, w_ffn_down))
    gmlp_bs_t = jnp.transpose(gmlp_bs, (0, 2, 1))
    for l in range(DEPTH):
        lam_init = 0.8 - 0.6 * math.exp(-0.3 * l)
        if xb is None:
            proj_ab, xb = _matmul(x, w_in, l, 0, AB_COLS, F32, t.proj_rows_f32, t.proj_cols_f32)
        else:
            proj_ab = _matmul(xb, w_in, l, 0, AB_COLS, F32, t.proj_rows, t.proj_cols)
        qkv = _matmul(xb, w_in, l, AB_COLS, QKV_COLS, BF16, t.proj_rows, t.proj_cols,
                      scaled_cols=DIFF_WIDTH, scale=ATTN_LOGIT_SCALE)
        y_ab = _mixer_ab(proj_ab, conv_w, gmlp_ln_g, gmlp_ln_b, gmlp_ws, gmlp_bs_t, l, seq,
                         t.mixer_rows)
        y_c = _diff_attention(qkv, bias_tiles, diff_lambda_q, diff_lambda_k, diff_subln_g, l,
                              lam_init, bsz, seq, t.attn, t.attn)
        x, xb = _matmul_res_ln([y_ab, y_c], w_mix_out, l, x, ln_mix_g, ln_mix_b, t.res_ln_rows)

        q_mem = _matmul(xb, w_mem_q, l, 0, D_MODEL, BF16, t.proj_rows, t.mem_q_cols)
        kv_mem = _matmul(memb, w_mem_kv, l, 0, 2 * D_MODEL, BF16, bsz * N_MEM, t.mem_kv_cols)
        o_mem = _mem_attention(q_mem, kv_mem, seq, t.mem_attn_rows)
        x, xb = _matmul_res_ln([o_mem], w_mem_out, l, x, ln_mem_g, ln_mem_b, t.res_ln_rows)

        x, xb = _ffn(xb, w_ffn_gu, w_ffn_down, l, x, ln_ffn_g, ln_ffn_b, t.ffn_rows, t.ffn_slab)
    return x.reshape(bsz, seq, d)
```

```python
import functools
import math

import jax
import jax.numpy as jnp
from jax import lax
from jax.experimental import pallas as pl
from jax.experimental.pallas import tpu as pltpu

D_MODEL = 2048
DEPTH = 2
N_MEM = 256
MEM_HEADS = 4
MEM_HEAD_DIM = D_MODEL // MEM_HEADS
CONV_WIDTH = D_MODEL // 4
CONV_K = 3
GMLP_WIDTH = D_MODEL // 4
GMLP_GROUPS = 4
GMLP_GROUP_DIM = GMLP_WIDTH // GMLP_GROUPS
CHUNK = 128
DIFF_HEAD_DIM = 128
DIFF_WIDTH = D_MODEL // 2
DIFF_HEADS = DIFF_WIDTH // (2 * DIFF_HEAD_DIM)
AB_COLS = 3 * CONV_WIDTH + 2 * GMLP_WIDTH
QKV_COLS = 3 * DIFF_WIDTH
NUM_BUCKETS = 32
MAX_DISTANCE = 128
D_FF = -(-8 * D_MODEL // (3 * 256)) * 256
ALPHA = (2 * DEPTH) ** 0.25
LN_EPS = 1e-5

V7X_VMEM_BYTES = 64 * 1024 * 1024
V7X_LANES = 128
V7X_SUBLANES = 8
VMEM_LIMIT_BYTES = V7X_VMEM_BYTES - 8 * 1024 * 1024

MASK_VALUE = -1e30
LOG2E = math.log2(math.e)
ATTN_LOGIT_SCALE = DIFF_HEAD_DIM ** -0.5 * LOG2E

F32 = jnp.float32
BF16 = jnp.bfloat16


def _params(*semantics):
    return pltpu.CompilerParams(dimension_semantics=semantics,
                                vmem_limit_bytes=VMEM_LIMIT_BYTES)


def _layer_norm_rows(x, g, b):
    mu = jnp.mean(x, axis=-1, keepdims=True)
    xc = x - mu
    var = jnp.mean(xc * xc, axis=-1, keepdims=True)
    return xc * lax.rsqrt(var + LN_EPS) * g + b


def _mm_kernel(a_ref, w_ref, o_ref, *maybe_ab_ref, scaled_tiles, scale):
    w = w_ref[...]
    if w.dtype != BF16:
        w = w.astype(BF16)
    if maybe_ab_ref:
        ab_ref, = maybe_ab_ref

        @pl.when(pl.program_id(1) == 0)
        def _():
            ab_ref[...] = a_ref[...].astype(BF16)

        a = ab_ref[...]
    else:
        a = a_ref[...]
    acc = jnp.dot(a, w, preferred_element_type=F32)
    if scaled_tiles:
        acc = acc * jnp.where(pl.program_id(1) < scaled_tiles, scale, 1.0)
    o_ref[...] = acc.astype(o_ref.dtype)


def _matmul(a, w, layer, col0, n, out_dtype, tm, tn, scaled_cols=0, scale=1.0):
    m, k = a.shape
    j0 = col0 // tn
    cast_a = a.dtype != BF16
    out_specs = [pl.BlockSpec((tm, tn), lambda i, j: (i, j))]
    out_shape = [jax.ShapeDtypeStruct((m, n), out_dtype)]
    if cast_a:
        out_specs.append(pl.BlockSpec((tm, k), lambda i, j: (i, 0)))
        out_shape.append(jax.ShapeDtypeStruct((m, k), BF16))
    outs = pl.pallas_call(
        functools.partial(_mm_kernel, scaled_tiles=scaled_cols // tn, scale=scale),
        grid=(m // tm, n // tn),
        in_specs=[pl.BlockSpec((tm, k), lambda i, j: (i, 0)),
                  pl.BlockSpec((None, k, tn), lambda i, j: (layer, 0, j0 + j))],
        out_specs=out_specs,
        out_shape=out_shape,
        compiler_params=_params("parallel", "arbitrary"),
        name="matmul",
    )(a, w)
    return outs if cast_a else outs[0]


def _mm_res_ln_kernel(*refs, n_a):
    a_refs = refs[:n_a]
    w_ref, res_ref, g_ref, b_ref, o_ref, ob_ref = refs[n_a:]
    a = jnp.concatenate([a_ref[...] for a_ref in a_refs], axis=1)
    acc = jnp.dot(a, w_ref[...], preferred_element_type=F32)
    y = _layer_norm_rows(ALPHA * res_ref[...] + acc, g_ref[...], b_ref[...])
    o_ref[...] = y
    ob_ref[...] = y.astype(BF16)


def _matmul_res_ln(a_list, w, layer, res, g, b, tm):
    m = res.shape[0]
    _, k, n = w.shape
    in_specs = [pl.BlockSpec((tm, a.shape[1]), lambda i: (i, 0)) for a in a_list]
    in_specs += [pl.BlockSpec((None, k, n), lambda i: (layer, 0, 0)),
                 pl.BlockSpec((tm, n), lambda i: (i, 0)),
                 pl.BlockSpec((None, 1, n), lambda i: (layer, 0, 0)),
                 pl.BlockSpec((None, 1, n), lambda i: (layer, 0, 0))]
    return pl.pallas_call(
        functools.partial(_mm_res_ln_kernel, n_a=len(a_list)),
        grid=(m // tm,),
        in_specs=in_specs,
        out_specs=[pl.BlockSpec((tm, n), lambda i: (i, 0)),
                   pl.BlockSpec((tm, n), lambda i: (i, 0))],
        out_shape=[jax.ShapeDtypeStruct((m, n), F32),
                   jax.ShapeDtypeStruct((m, n), BF16)],
        compiler_params=_params("parallel"),
        name="matmul_res_ln",
    )(*a_list, w, res, g.reshape(-1, 1, n), b.reshape(-1, 1, n))


def _mixer_ab_kernel(bg_ref, cg_ref, h_ref, u_ref, v_ref, cgp_ref, hp_ref,
                     cw_ref, lg_ref, lb_ref, ws_ref, bst_ref, o_ref, *, tm, tiles_per_seq):
    i = pl.program_id(0)
    z = cg_ref[...] * h_ref[...]
    zp = cgp_ref[...] * hp_ref[...]
    zp = jnp.where(i % tiles_per_seq == 0, 0.0, zp)
    rows = lax.broadcasted_iota(jnp.int32, (tm, 1), 0)
    last = zp[V7X_SUBLANES - 1:V7X_SUBLANES, :]
    last2 = zp[V7X_SUBLANES - 2:V7X_SUBLANES - 1, :]
    z1 = jnp.where(rows == 0, last, pltpu.roll(z, 1, axis=0))
    z2 = jnp.where(rows == 0, last2, jnp.where(rows == 1, last, pltpu.roll(z, 2, axis=0)))
    cw = cw_ref[...]
    conv = cw[0:1, :] * z2 + cw[1:2, :] * z1 + cw[2:3, :] * z
    o_ref[:, 0:CONV_WIDTH] = (bg_ref[...] * conv).astype(o_ref.dtype)

    u = jax.nn.gelu(u_ref[...])
    vn = _layer_norm_rows(jax.nn.gelu(v_ref[...]), lg_ref[...], lb_ref[...]).astype(BF16)
    tr = lax.broadcasted_iota(jnp.int32, (CHUNK, CHUNK), 0)
    tc = lax.broadcasted_iota(jnp.int32, (CHUNK, CHUNK), 1)
    causal = tr >= tc
    bst = bst_ref[...]
    for g in range(GMLP_GROUPS):
        wg = jnp.where(causal, ws_ref[g], 0.0).astype(BF16)
        bias = bst[:, g:g + 1]
        c0 = g * GMLP_GROUP_DIM
        for c in range(tm // CHUNK):
            r0 = c * CHUNK
            sv = jnp.dot(wg, vn[r0:r0 + CHUNK, c0:c0 + GMLP_GROUP_DIM],
                         preferred_element_type=F32) + bias
            o_ref[r0:r0 + CHUNK, CONV_WIDTH + c0:CONV_WIDTH + c0 + GMLP_GROUP_DIM] = (
                u[r0:r0 + CHUNK, c0:c0 + GMLP_GROUP_DIM] * sv).astype(o_ref.dtype)


def _mixer_ab(proj_ab, conv_w, ln_g, ln_b, ws, bs_t, layer, seq, tm):
    m = proj_ab.shape[0]
    w = CONV_WIDTH
    halo_blocks = tm // V7X_SUBLANES

    def col(c):
        return pl.BlockSpec((tm, w), lambda i, c=c: (i, c))

    def halo(c):
        return pl.BlockSpec((V7X_SUBLANES, w),
                            lambda i, c=c: (jnp.maximum(i * halo_blocks - 1, 0), c))

    def whole(shape):
        return pl.BlockSpec((None,) + shape, lambda i: (layer,) + (0,) * len(shape))

    return pl.pallas_call(
        functools.partial(_mixer_ab_kernel, tm=tm, tiles_per_seq=seq // tm),
        grid=(m // tm,),
        in_specs=[col(0), col(1), col(2), col(3), col(4), halo(1), halo(2),
                  whole((CONV_K, w)), whole((1, w)), whole((1, w)),
                  whole((GMLP_GROUPS, CHUNK, CHUNK)), whole((CHUNK, GMLP_GROUPS))],
        out_specs=pl.BlockSpec((tm, 2 * w), lambda i: (i, 0)),
        out_shape=jax.ShapeDtypeStruct((m, 2 * w), BF16),
        compiler_params=_params("parallel"),
        name="mixer_ab",
    )(proj_ab, proj_ab, proj_ab, proj_ab, proj_ab, proj_ab, proj_ab,
      conv_w, ln_g.reshape(-1, 1, w), ln_b.reshape(-1, 1, w), ws, bs_t)


def _causal_bucket(n):
    max_exact = NUM_BUCKETS // 2
    nf = jnp.maximum(n, 1).astype(F32)
    large = max_exact + (jnp.log(nf / max_exact) / math.log(MAX_DISTANCE / max_exact)
                         * (NUM_BUCKETS - max_exact)).astype(jnp.int32)
    large = jnp.minimum(large, NUM_BUCKETS - 1)
    return jnp.where(n < max_exact, n, large)


def _near_bias_tiles(rel_bias, tq, tk):
    r = tq // tk
    rb = rel_bias.astype(F32)
    rb = jnp.transpose((rb - rb[NUM_BUCKETS - 1]) * LOG2E).reshape(DIFF_HEADS, 2, NUM_BUCKETS)

    def lookup(dist):
        bucket = _causal_bucket(jnp.maximum(dist, 0))
        tiles = jnp.zeros((DIFF_HEADS, 2) + dist.shape, F32)
        for b in range(NUM_BUCKETS - 1):
            tiles = jnp.where(bucket == b, rb[:, :, b].reshape((DIFF_HEADS, 2) + (1,) * dist.ndim),
                              tiles)
        return jnp.where(dist >= 0, tiles, MASK_VALUE)

    qpos = jnp.arange(tq)[None, :]
    kpos = jnp.arange(tk)[:, None]
    diag = lookup(jnp.stack([qpos - kpos - e * tk for e in range(r)], axis=0))
    c = MAX_DISTANCE
    corner = lookup(jnp.arange(c)[None, :] - jnp.arange(c)[:, None] + c)
    return diag, corner


def _diff_attn_kernel(q_ref, qn_ref, k_ref, v_ref, bias_ref, corner_ref, lq_ref, lk_ref, g_ref, o_ref,
                      acc_ref, m_ref, l_ref, t_ref, pm_ref, qt_ref, *, lam_init, tq, tk):
    qi = pl.program_id(2)
    dh = DIFF_HEAD_DIM
    sub = V7X_SUBLANES
    acc_ref[...] = jnp.zeros_like(acc_ref)
    m_ref[...] = jnp.full_like(m_ref, MASK_VALUE)
    l_ref[...] = jnp.zeros_like(l_ref)

    def fold_keys(x, op):
        out = x[0:sub]
        for i in range(1, x.shape[0] // sub):
            out = op(out, x[i * sub:(i + 1) * sub])
        return out

    def scores(j, slot, q_index=qi):
        start = pl.multiple_of(j * tk, tk)
        is_below = (j == q_index - 1).astype(F32)
        c = MAX_DISTANCE
        for mp in range(2):
            k = k_ref[pl.ds(start, tk), mp * dh:(mp + 1) * dh]
            t = jnp.dot(k, qt_ref[mp], preferred_element_type=F32)
            head, tail = t[0:tk - c], t[tk - c:tk]
            tail = jnp.concatenate([tail[:, 0:c] + is_below * corner_ref[mp], tail[:, c:]], axis=1)
            t_ref[slot, mp, 0:tk - c, :] = head
            t_ref[slot, mp, tk - c:tk, :] = tail
            pm_ref[slot, mp] = jnp.maximum(fold_keys(head, jnp.maximum),
                                           fold_keys(tail, jnp.maximum))

    def accumulate(j, slot, bias_tile):
        start = pl.multiple_of(j * tk, tk)
        v_t = jnp.transpose(v_ref[pl.ds(start, tk), :])
        for mp in range(2):
            t = t_ref[slot, mp]
            if bias_tile is None:
                pm = pm_ref[slot, mp]
            else:
                t = t + bias_ref[mp, bias_tile]
                pm = fold_keys(t, jnp.maximum)
            m_prev = m_ref[mp]
            m_new = jnp.maximum(m_prev, jnp.max(pm, axis=0, keepdims=True))
            p = jnp.exp2(t - m_new)
            a = jnp.exp2(m_prev - m_new)
            l_ref[mp] = a * l_ref[mp] + fold_keys(p, jnp.add)
            m_ref[mp] = m_new
            pv = jnp.dot(v_t, p.astype(BF16), preferred_element_type=F32)
            acc_ref[mp] = a * acc_ref[mp] + pv

    def pipelined(j, count):
        for n in range(count):
            scores(j + n + 1, (n + 1) % 2)
            accumulate(j + n, n % 2, None)

    def prepare(src_ref, q_index):
        for mp in range(2):
            q = src_ref[:, mp * dh:(mp + 1) * dh].astype(F32)
            qt_ref[mp] = jnp.transpose(q).astype(BF16)
        scores(0, 0, q_index)

    @pl.when(qi == 0)
    def _():
        prepare(q_ref, qi)

    quads = qi // 4
    rem = qi % 4

    @pl.loop(0, quads)
    def _(i):
        pipelined(4 * i, 4)

    @pl.when(rem >= 2)
    def _():
        pipelined(4 * quads, 2)

    @pl.when(rem % 2 == 0)
    def _():
        accumulate(qi, 0, 0)

    @pl.when(rem % 2 == 1)
    def _():
        scores(qi, 1)
        accumulate(qi - 1, 0, None)
        accumulate(qi, 1, 0)

    prepare(qn_ref, qi + 1)
    outs = []
    for mp in range(2):
        l = jnp.sum(l_ref[mp], axis=0, keepdims=True)
        outs.append(acc_ref[mp] / l)
    sums = jnp.sum(lq_ref[...] * lk_ref[...], axis=-1, keepdims=True)
    ex = jnp.exp(sums)
    lam = ex[0:1, :] - ex[1:2, :] + lam_init
    o = jnp.transpose(outs[0] - lam * outs[1])
    o = o * lax.rsqrt(jnp.mean(o * o, axis=-1, keepdims=True) + LN_EPS) * g_ref[...]
    o_ref[...] = (o * (1.0 - lam_init)).astype(o_ref.dtype)


def _diff_attention(qkv, bias_tiles, lam_q, lam_k, subln_g, layer, lam_init, bsz, seq, tq, tk):
    m = qkv.shape[0]
    hd = 2 * DIFF_HEAD_DIM
    nq = seq // tq
    assert tq == tk, "the block schedule assumes square score tiles"
    r = tq // tk
    diag, corner = bias_tiles
    c = MAX_DISTANCE
    return pl.pallas_call(
        functools.partial(_diff_attn_kernel, lam_init=lam_init, tq=tq, tk=tk),
        grid=(bsz, DIFF_HEADS, nq),
        in_specs=[
            pl.BlockSpec((tq, hd), lambda b, h, i: (b * nq + i, h)),
            pl.BlockSpec((tq, hd), lambda b, h, i: (b * nq + jnp.minimum(i + 1, nq - 1), h)),
            pl.BlockSpec((seq, hd), lambda b, h, i: (b, DIFF_HEADS + h)),
            pl.BlockSpec((seq, hd), lambda b, h, i: (b, 2 * DIFF_HEADS + h)),
            pl.BlockSpec((None, 2, r, tk, tq), lambda b, h, i: (h, 0, 0, 0, 0)),
            pl.BlockSpec((None, 2, c, c), lambda b, h, i: (h, 0, 0, 0)),
            pl.BlockSpec((None, 2, DIFF_HEAD_DIM), lambda b, h, i: (layer, 0, 0)),
            pl.BlockSpec((None, 2, DIFF_HEAD_DIM), lambda b, h, i: (layer, 0, 0)),
            pl.BlockSpec((None, 1, hd), lambda b, h, i: (layer, 0, 0)),
        ],
        out_specs=pl.BlockSpec((tq, hd), lambda b, h, i: (b * nq + i, h)),
        out_shape=jax.ShapeDtypeStruct((m, DIFF_WIDTH), BF16),
        scratch_shapes=[pltpu.VMEM((2, hd, tq), F32),
                        pltpu.VMEM((2, 1, tq), F32),
                        pltpu.VMEM((2, V7X_SUBLANES, tq), F32),
                        pltpu.VMEM((2, 2, tk, tq), F32),
                        pltpu.VMEM((2, 2, V7X_SUBLANES, tq), F32),
                        pltpu.VMEM((2, DIFF_HEAD_DIM, tq), BF16)],
        compiler_params=_params("parallel", "parallel", "arbitrary"),
        name="diff_attention",
    )(qkv, qkv, qkv, qkv, diag, corner, lam_q, lam_k, subln_g.reshape(-1, 1, hd))


def _mem_attn_kernel(q_ref, kv_ref, o_ref):
    d = MEM_HEAD_DIM
    scale = d ** -0.5
    for h in range(MEM_HEADS):
        q = q_ref[:, h * d:(h + 1) * d]
        k = kv_ref[:, h * d:(h + 1) * d]
        v = kv_ref[:, D_MODEL + h * d:D_MODEL + (h + 1) * d]
        s = lax.dot_general(q, k, (((1,), (1,)), ((), ())), preferred_element_type=F32) * scale
        s = s - jnp.max(s, axis=-1, keepdims=True)
        p = jnp.exp(s)
        p = p / jnp.sum(p, axis=-1, keepdims=True)
        o_ref[:, h * d:(h + 1) * d] = jnp.dot(p.astype(BF16), v,
                                              preferred_element_type=F32).astype(o_ref.dtype)


def _mem_attention(q, kv, seq, tm):
    m = q.shape[0]
    tiles_per_seq = seq // tm
    return pl.pallas_call(
        _mem_attn_kernel,
        grid=(m // tm,),
        in_specs=[pl.BlockSpec((tm, D_MODEL), lambda i: (i, 0)),
                  pl.BlockSpec((N_MEM, 2 * D_MODEL), lambda i: (i // tiles_per_seq, 0))],
        out_specs=pl.BlockSpec((tm, D_MODEL), lambda i: (i, 0)),
        out_shape=jax.ShapeDtypeStruct((m, D_MODEL), BF16),
        compiler_params=_params("parallel"),
        name="mem_attention",
    )(q, kv)


FFN_HALF_ROWS = 512
FFN_RES_FETCH_SLAB = 4


def _ffn_kernel(xb_ref, wg_ref, wu_ref, wd_ref, res_hbm, g_ref, b_ref, o_hbm, ob_hbm,
                acc_ref, rbuf_ref, yb_ref, sems, *, tm):
    i = pl.program_id(0)
    f = pl.program_id(1)
    last_i = pl.num_programs(0) - 1
    last_f = pl.num_programs(1) - 1

    def rows_of(tile):
        return pl.ds(pl.multiple_of(tile * tm, tm), tm)

    def res_copy(tile):
        return pltpu.make_async_copy(res_hbm.at[rows_of(tile), :], rbuf_ref, sems.at[0])

    def out_copies(tile):
        return (pltpu.make_async_copy(rbuf_ref, o_hbm.at[rows_of(tile), :], sems.at[1]),
                pltpu.make_async_copy(yb_ref, ob_hbm.at[rows_of(tile), :], sems.at[2]))

    @pl.when(f == FFN_RES_FETCH_SLAB)
    def _():
        @pl.when(i > 0)
        def _():
            for c in out_copies(i - 1):
                c.wait()

        res_copy(i).start()

    def slab(first):
        for r0 in range(0, tm, FFN_HALF_ROWS):
            rows = slice(r0, r0 + FFN_HALF_ROWS)
            xb = xb_ref[rows, :]
            gate = jnp.dot(xb, wg_ref[...], preferred_element_type=F32)
            up = jnp.dot(xb, wu_ref[...], preferred_element_type=F32)
            hid = (gate * (1.0 / (1.0 + jnp.exp(-gate))) * up).astype(BF16)
            part = jnp.dot(hid, wd_ref[...], preferred_element_type=F32)
            acc_ref[rows, :] = part if first else acc_ref[rows, :] + part

    @pl.when(f == 0)
    def _():
        slab(True)

    @pl.when(f > 0)
    def _():
        slab(False)

    @pl.when(f == last_f)
    def _():
        res_copy(i).wait()
        y = _layer_norm_rows(ALPHA * rbuf_ref[...] + acc_ref[...], g_ref[...], b_ref[...])
        rbuf_ref[...] = y
        yb_ref[...] = y.astype(BF16)
        for c in out_copies(i):
            c.start()

        @pl.when(i == last_i)
        def _():
            for c in out_copies(i):
                c.wait()


def _ffn(xb, w_gu, w_down, layer, res, g, b, tm, tf):
    m, d = xb.shape
    nf = D_FF // tf
    assert FFN_RES_FETCH_SLAB < nf - 1 and tm % FFN_HALF_ROWS == 0
    any_spec = pl.BlockSpec(memory_space=pl.ANY)
    return pl.pallas_call(
        functools.partial(_ffn_kernel, tm=tm),
        grid=(m // tm, nf),
        in_specs=[pl.BlockSpec((tm, d), lambda i, f: (i, 0)),
                  pl.BlockSpec((None, d, tf), lambda i, f: (layer, 0, f)),
                  pl.BlockSpec((None, d, tf), lambda i, f: (layer, 0, nf + f)),
                  pl.BlockSpec((None, tf, d), lambda i, f: (layer, f, 0)),
                  any_spec,
                  pl.BlockSpec((None, 1, d), lambda i, f: (layer, 0, 0)),
                  pl.BlockSpec((None, 1, d), lambda i, f: (layer, 0, 0))],
        out_specs=[any_spec, any_spec],
        out_shape=[jax.ShapeDtypeStruct((m, d), F32),
                   jax.ShapeDtypeStruct((m, d), BF16)],
        scratch_shapes=[pltpu.VMEM((tm, d), F32),
                        pltpu.VMEM((tm, d), F32),
                        pltpu.VMEM((tm, d), BF16),
                        pltpu.SemaphoreType.DMA((3,))],
        compiler_params=_params("arbitrary", "arbitrary"),
        name="ffn",
    )(xb, w_gu, w_gu, w_down, res, g.reshape(-1, 1, d), b.reshape(-1, 1, d))


class _Tiles:
    proj_rows = 2048
    proj_rows_f32 = 1024
    proj_cols_f32 = 1280
    proj_cols = 512
    mem_q_cols = 1024
    mem_kv_cols = 1024
    res_ln_rows = 512
    mixer_rows = 1024
    mem_attn_rows = 2048
    ffn_rows = 1024
    ffn_slab = 512
    attn = 512


def kernel(x, mem, rel_bias, w_in, conv_w, gmlp_ln_g, gmlp_ln_b, gmlp_ws, gmlp_bs,
           diff_lambda_q, diff_lambda_k, diff_subln_g, w_mix_out, ln_mix_g, ln_mix_b,
           w_mem_q, w_mem_kv, w_mem_out, ln_mem_g, ln_mem_b,
           w_ffn_gu, w_ffn_down, ln_ffn_g, ln_ffn_b):
    bsz, seq, d = x.shape
    m = bsz * seq
    x = x.reshape(m, d)
    xb = None
    memb = mem.reshape(bsz * N_MEM, d).astype(BF16)
    t = _Tiles
    bias_tiles = _near_bias_tiles(rel_bias, t.attn, t.attn)
    w_in, w_mix_out, w_mem_q, w_mem_out, w_ffn_gu, w_ffn_down = (
        w.astype(BF16) for w in (w_in, w_mix_out, w_mem_q, w_mem_out, w_ffn_gu, w_ffn_down))
    gmlp_bs_t = jnp.transpose(gmlp_bs, (0, 2, 1))
    for l in range(DEPTH):
        lam_init = 0.8 - 0.6 * math.exp(-0.3 * l)
        if xb is None:
            proj_ab, xb = _matmul(x, w_in, l, 0, AB_COLS, F32, t.proj_rows_f32, t.proj_cols_f32)
        else:
            proj_ab = _matmul(xb, w_in, l, 0, AB_COLS, F32, t.proj_rows, t.proj_cols)
        qkv = _matmul(xb, w_in, l, AB_COLS, QKV_COLS, BF16, t.proj_rows, t.proj_cols,
                      scaled_cols=DIFF_WIDTH, scale=ATTN_LOGIT_SCALE)
        y_ab = _mixer_ab(proj_ab, conv_w, gmlp_ln_g, gmlp_ln_b, gmlp_ws, gmlp_bs_t, l, seq,
                         t.mixer_rows)
        y_c = _diff_attention(qkv, bias_tiles, diff_lambda_q, diff_lambda_k, diff_subln_g, l,
                              lam_init, bsz, seq, t.attn, t.attn)
        x, xb = _matmul_res_ln([y_ab, y_c], w_mix_out, l, x, ln_mix_g, ln_mix_b, t.res_ln_rows)

        q_mem = _matmul(xb, w_mem_q, l, 0, D_MODEL, BF16, t.proj_rows, t.mem_q_cols)
        kv_mem = _matmul(memb, w_mem_kv, l, 0, 2 * D_MODEL, BF16, bsz * N_MEM, t.mem_kv_cols)
        o_mem = _mem_attention(q_mem, kv_mem, seq, t.mem_attn_rows)
        x, xb = _matmul_res_ln([o_mem], w_mem_out, l, x, ln_mem_g, ln_mem_b, t.res_ln_rows)

        x, xb = _ffn(xb, w_ffn_gu, w_ffn_down, l, x, ln_ffn_g, ln_ffn_b, t.ffn_rows, t.ffn_slab)
    return x.reshape(bsz, seq, d)
```

```python
import functools
import math

import jax
import jax.numpy as jnp
from jax import lax
from jax.experimental import pallas as pl
from jax.experimental.pallas import tpu as pltpu

D_MODEL = 2048
DEPTH = 2
N_MEM = 256
MEM_HEADS = 4
MEM_HEAD_DIM = D_MODEL // MEM_HEADS
CONV_WIDTH = D_MODEL // 4
CONV_K = 3
GMLP_WIDTH = D_MODEL // 4
GMLP_GROUPS = 4
GMLP_GROUP_DIM = GMLP_WIDTH // GMLP_GROUPS
CHUNK = 128
DIFF_HEAD_DIM = 128
DIFF_WIDTH = D_MODEL // 2
DIFF_HEADS = DIFF_WIDTH // (2 * DIFF_HEAD_DIM)
AB_COLS = 3 * CONV_WIDTH + 2 * GMLP_WIDTH
QKV_COLS = 3 * DIFF_WIDTH
NUM_BUCKETS = 32
MAX_DISTANCE = 128
D_FF = -(-8 * D_MODEL // (3 * 256)) * 256
ALPHA = (2 * DEPTH) ** 0.25
LN_EPS = 1e-5

V7X_VMEM_BYTES = 64 * 1024 * 1024
V7X_LANES = 128
V7X_SUBLANES = 8
VMEM_LIMIT_BYTES = V7X_VMEM_BYTES - 8 * 1024 * 1024

MASK_VALUE = -1e30
LOG2E = math.log2(math.e)
ATTN_LOGIT_SCALE = DIFF_HEAD_DIM ** -0.5 * LOG2E

F32 = jnp.float32
BF16 = jnp.bfloat16


def _params(*semantics):
    return pltpu.CompilerParams(dimension_semantics=semantics,
                                vmem_limit_bytes=VMEM_LIMIT_BYTES)


def _layer_norm_rows(x, g, b):
    mu = jnp.mean(x, axis=-1, keepdims=True)
    xc = x - mu
    var = jnp.mean(xc * xc, axis=-1, keepdims=True)
    return xc * lax.rsqrt(var + LN_EPS) * g + b


def _mm_kernel(a_ref, w_ref, o_ref, *cast_refs, scaled_tiles, scale, tm):
    w = w_ref[...]
    if w.dtype != BF16:
        w = w.astype(BF16)
    if cast_refs:
        ab_ref, abuf_ref, sems = cast_refs
        i = pl.program_id(0)

        def a_copy(tile, slot):
            rows = pl.ds(pl.multiple_of(tile * tm, tm), tm)
            return pltpu.make_async_copy(a_ref.at[rows, :], abuf_ref.at[slot], sems.at[slot])

        @pl.when(pl.program_id(1) == 0)
        def _():
            slot = i % 2

            @pl.when(i == 0)
            def _():
                a_copy(0, 0).start()

            @pl.when(i + 1 < pl.num_programs(0))
            def _():
                a_copy(i + 1, 1 - slot).start()

            a_copy(i, slot).wait()
            ab_ref[...] = abuf_ref[slot].astype(BF16)

        a = ab_ref[...]
    else:
        a = a_ref[...]
    acc = jnp.dot(a, w, preferred_element_type=F32)
    if scaled_tiles:
        acc = acc * jnp.where(pl.program_id(1) < scaled_tiles, scale, 1.0)
    o_ref[...] = acc.astype(o_ref.dtype)


def _matmul(a, w, layer, col0, n, out_dtype, tm, tn, scaled_cols=0, scale=1.0):
    m, k = a.shape
    j0 = col0 // tn
    cast_a = a.dtype != BF16
    a_spec = pl.BlockSpec((tm, k), lambda i, j: (i, 0))
    out_specs = [pl.BlockSpec((tm, tn), lambda i, j: (i, j))]
    out_shape = [jax.ShapeDtypeStruct((m, n), out_dtype)]
    scratch = []
    if cast_a:
        a_spec = pl.BlockSpec(memory_space=pl.ANY)
        out_specs.append(pl.BlockSpec((tm, k), lambda i, j: (i, 0)))
        out_shape.append(jax.ShapeDtypeStruct((m, k), BF16))
        scratch = [pltpu.VMEM((2, tm, k), a.dtype), pltpu.SemaphoreType.DMA((2,))]
    outs = pl.pallas_call(
        functools.partial(_mm_kernel, scaled_tiles=scaled_cols // tn, scale=scale, tm=tm),
        grid=(m // tm, n // tn),
        in_specs=[a_spec,
                  pl.BlockSpec((None, k, tn), lambda i, j: (layer, 0, j0 + j))],
        out_specs=out_specs,
        out_shape=out_shape,
        scratch_shapes=scratch,
        compiler_params=_params("arbitrary", "arbitrary"),
        name="matmul",
    )(a, w)
    return outs if cast_a else outs[0]


def _mm_res_ln_kernel(*refs, n_a):
    a_refs = refs[:n_a]
    w_ref, res_ref, g_ref, b_ref, o_ref, ob_ref = refs[n_a:]
    a = jnp.concatenate([a_ref[...] for a_ref in a_refs], axis=1)
    acc = jnp.dot(a, w_ref[...], preferred_element_type=F32)
    y = _layer_norm_rows(ALPHA * res_ref[...] + acc, g_ref[...], b_ref[...])
    o_ref[...] = y
    ob_ref[...] = y.astype(BF16)


def _matmul_res_ln(a_list, w, layer, res, g, b, tm):
    m = res.shape[0]
    _, k, n = w.shape
    in_specs = [pl.BlockSpec((tm, a.shape[1]), lambda i: (i, 0)) for a in a_list]
    in_specs += [pl.BlockSpec((None, k, n), lambda i: (layer, 0, 0)),
                 pl.BlockSpec((tm, n), lambda i: (i, 0)),
                 pl.BlockSpec((None, 1, n), lambda i: (layer, 0, 0)),
                 pl.BlockSpec((None, 1, n), lambda i: (layer, 0, 0))]
    return pl.pallas_call(
        functools.partial(_mm_res_ln_kernel, n_a=len(a_list)),
        grid=(m // tm,),
        in_specs=in_specs,
        out_specs=[pl.BlockSpec((tm, n), lambda i: (i, 0)),
                   pl.BlockSpec((tm, n), lambda i: (i, 0))],
        out_shape=[jax.ShapeDtypeStruct((m, n), F32),
                   jax.ShapeDtypeStruct((m, n), BF16)],
        compiler_params=_params("parallel"),
        name="matmul_res_ln",
    )(*a_list, w, res, g.reshape(-1, 1, n), b.reshape(-1, 1, n))


def _mixer_ab_kernel(bg_ref, cg_ref, h_ref, u_ref, v_ref, cgp_ref, hp_ref,
                     cw_ref, lg_ref, lb_ref, ws_ref, bst_ref, o_ref, *, tm, tiles_per_seq):
    i = pl.program_id(0)
    z = cg_ref[...] * h_ref[...]
    zp = cgp_ref[...] * hp_ref[...]
    zp = jnp.where(i % tiles_per_seq == 0, 0.0, zp)
    rows = lax.broadcasted_iota(jnp.int32, (tm, 1), 0)
    last = zp[V7X_SUBLANES - 1:V7X_SUBLANES, :]
    last2 = zp[V7X_SUBLANES - 2:V7X_SUBLANES - 1, :]
    z1 = jnp.where(rows == 0, last, pltpu.roll(z, 1, axis=0))
    z2 = jnp.where(rows == 0, last2, jnp.where(rows == 1, last, pltpu.roll(z, 2, axis=0)))
    cw = cw_ref[...]
    conv = cw[0:1, :] * z2 + cw[1:2, :] * z1 + cw[2:3, :] * z
    o_ref[:, 0:CONV_WIDTH] = (bg_ref[...] * conv).astype(o_ref.dtype)

    u = jax.nn.gelu(u_ref[...])
    vn = _layer_norm_rows(jax.nn.gelu(v_ref[...]), lg_ref[...], lb_ref[...]).astype(BF16)
    tr = lax.broadcasted_iota(jnp.int32, (CHUNK, CHUNK), 0)
    tc = lax.broadcasted_iota(jnp.int32, (CHUNK, CHUNK), 1)
    causal = tr >= tc
    bst = bst_ref[...]
    for g in range(GMLP_GROUPS):
        wg = jnp.where(causal, ws_ref[g], 0.0).astype(BF16)
        bias = bst[:, g:g + 1]
        c0 = g * GMLP_GROUP_DIM
        for c in range(tm // CHUNK):
            r0 = c * CHUNK
            sv = jnp.dot(wg, vn[r0:r0 + CHUNK, c0:c0 + GMLP_GROUP_DIM],
                         preferred_element_type=F32) + bias
            o_ref[r0:r0 + CHUNK, CONV_WIDTH + c0:CONV_WIDTH + c0 + GMLP_GROUP_DIM] = (
                u[r0:r0 + CHUNK, c0:c0 + GMLP_GROUP_DIM] * sv).astype(o_ref.dtype)


def _mixer_ab(proj_ab, conv_w, ln_g, ln_b, ws, bs_t, layer, seq, tm):
    m = proj_ab.shape[0]
    w = CONV_WIDTH
    halo_blocks = tm // V7X_SUBLANES

    def col(c):
        return pl.BlockSpec((tm, w), lambda i, c=c: (i, c))

    def halo(c):
        return pl.BlockSpec((V7X_SUBLANES, w),
                            lambda i, c=c: (jnp.maximum(i * halo_blocks - 1, 0), c))

    def whole(shape):
        return pl.BlockSpec((None,) + shape, lambda i: (layer,) + (0,) * len(shape))

    return pl.pallas_call(
        functools.partial(_mixer_ab_kernel, tm=tm, tiles_per_seq=seq // tm),
        grid=(m // tm,),
        in_specs=[col(0), col(1), col(2), col(3), col(4), halo(1), halo(2),
                  whole((CONV_K, w)), whole((1, w)), whole((1, w)),
                  whole((GMLP_GROUPS, CHUNK, CHUNK)), whole((CHUNK, GMLP_GROUPS))],
        out_specs=pl.BlockSpec((tm, 2 * w), lambda i: (i, 0)),
        out_shape=jax.ShapeDtypeStruct((m, 2 * w), BF16),
        compiler_params=_params("parallel"),
        name="mixer_ab",
    )(proj_ab, proj_ab, proj_ab, proj_ab, proj_ab, proj_ab, proj_ab,
      conv_w, ln_g.reshape(-1, 1, w), ln_b.reshape(-1, 1, w), ws, bs_t)


def _causal_bucket(n):
    max_exact = NUM_BUCKETS // 2
    nf = jnp.maximum(n, 1).astype(F32)
    large = max_exact + (jnp.log(nf / max_exact) / math.log(MAX_DISTANCE / max_exact)
                         * (NUM_BUCKETS - max_exact)).astype(jnp.int32)
    large = jnp.minimum(large, NUM_BUCKETS - 1)
    return jnp.where(n < max_exact, n, large)


def _near_bias_tiles(rel_bias, tq, tk):
    r = tq // tk
    rb = rel_bias.astype(F32)
    rb = jnp.transpose((rb - rb[NUM_BUCKETS - 1]) * LOG2E).reshape(DIFF_HEADS, 2, NUM_BUCKETS)

    def lookup(dist):
        bucket = _causal_bucket(jnp.maximum(dist, 0))
        tiles = jnp.zeros((DIFF_HEADS, 2) + dist.shape, F32)
        for b in range(NUM_BUCKETS - 1):
            tiles = jnp.where(bucket == b, rb[:, :, b].reshape((DIFF_HEADS, 2) + (1,) * dist.ndim),
                              tiles)
        return jnp.where(dist >= 0, tiles, MASK_VALUE)

    qpos = jnp.arange(tq)[None, :]
    kpos = jnp.arange(tk)[:, None]
    diag = lookup(jnp.stack([qpos - kpos - e * tk for e in range(r)], axis=0))
    c = MAX_DISTANCE
    corner = lookup(jnp.arange(c)[None, :] - jnp.arange(c)[:, None] + c)
    return diag, corner


def _diff_attn_kernel(q_ref, qn_ref, k_ref, v_ref, bias_ref, corner_ref, lq_ref, lk_ref, g_ref, o_ref,
                      acc_ref, m_ref, l_ref, t_ref, pm_ref, qt_ref, *, lam_init, tq, tk):
    qi = pl.program_id(2)
    dh = DIFF_HEAD_DIM
    sub = V7X_SUBLANES
    acc_ref[...] = jnp.zeros_like(acc_ref)
    m_ref[...] = jnp.full_like(m_ref, MASK_VALUE)
    l_ref[...] = jnp.zeros_like(l_ref)

    def fold_keys(x, op):
        out = x[0:sub]
        for i in range(1, x.shape[0] // sub):
            out = op(out, x[i * sub:(i + 1) * sub])
        return out

    def scores(j, slot, q_index=qi):
        start = pl.multiple_of(j * tk, tk)
        is_below = (j == q_index - 1).astype(F32)
        c = MAX_DISTANCE
        for mp in range(2):
            k = k_ref[pl.ds(start, tk), mp * dh:(mp + 1) * dh]
            t = jnp.dot(k, qt_ref[mp], preferred_element_type=F32)
            head, tail = t[0:tk - c], t[tk - c:tk]
            tail = jnp.concatenate([tail[:, 0:c] + is_below * corner_ref[mp], tail[:, c:]], axis=1)
            t_ref[slot, mp, 0:tk - c, :] = head
            t_ref[slot, mp, tk - c:tk, :] = tail
            pm_ref[slot, mp] = jnp.maximum(fold_keys(head, jnp.maximum),
                                           fold_keys(tail, jnp.maximum))

    def accumulate(j, slot, bias_tile):
        start = pl.multiple_of(j * tk, tk)
        v_t = jnp.transpose(v_ref[pl.ds(start, tk), :])
        for mp in range(2):
            t = t_ref[slot, mp]
            if bias_tile is None:
                pm = pm_ref[slot, mp]
            else:
                t = t + bias_ref[mp, bias_tile]
                pm = fold_keys(t, jnp.maximum)
            m_prev = m_ref[mp]
            m_new = jnp.maximum(m_prev, jnp.max(pm, axis=0, keepdims=True))
            p = jnp.exp2(t - m_new)
            a = jnp.exp2(m_prev - m_new)
            l_ref[mp] = a * l_ref[mp] + fold_keys(p, jnp.add)
            m_ref[mp] = m_new
            pv = jnp.dot(v_t, p.astype(BF16), preferred_element_type=F32)
            acc_ref[mp] = a * acc_ref[mp] + pv

    def pipelined(j, count):
        for n in range(count):
            scores(j + n + 1, (n + 1) % 2)
            accumulate(j + n, n % 2, None)

    def prepare(src_ref, q_index):
        for mp in range(2):
            q = src_ref[:, mp * dh:(mp + 1) * dh].astype(F32)
            qt_ref[mp] = jnp.transpose(q).astype(BF16)
        scores(0, 0, q_index)

    @pl.when(qi == 0)
    def _():
        prepare(q_ref, qi)

    quads = qi // 4
    rem = qi % 4

    @pl.loop(0, quads)
    def _(i):
        pipelined(4 * i, 4)

    @pl.when(rem >= 2)
    def _():
        pipelined(4 * quads, 2)

    @pl.when(rem % 2 == 0)
    def _():
        accumulate(qi, 0, 0)

    @pl.when(rem % 2 == 1)
    def _():
        scores(qi, 1)
        accumulate(qi - 1, 0, None)
        accumulate(qi, 1, 0)

    prepare(qn_ref, qi + 1)
    outs = []
    for mp in range(2):
        l = jnp.sum(l_ref[mp], axis=0, keepdims=True)
        outs.append(acc_ref[mp] / l)
    sums = jnp.sum(lq_ref[...] * lk_ref[...], axis=-1, keepdims=True)
    ex = jnp.exp(sums)
    lam = ex[0:1, :] - ex[1:2, :] + lam_init
    o = jnp.transpose(outs[0] - lam * outs[1])
    o = o * lax.rsqrt(jnp.mean(o * o, axis=-1, keepdims=True) + LN_EPS) * g_ref[...]
    o_ref[...] = (o * (1.0 - lam_init)).astype(o_ref.dtype)


def _diff_attention(qkv, bias_tiles, lam_q, lam_k, subln_g, layer, lam_init, bsz, seq, tq, tk):
    m = qkv.shape[0]
    hd = 2 * DIFF_HEAD_DIM
    nq = seq // tq
    assert tq == tk, "the block schedule assumes square score tiles"
    r = tq // tk
    diag, corner = bias_tiles
    c = MAX_DISTANCE
    return pl.pallas_call(
        functools.partial(_diff_attn_kernel, lam_init=lam_init, tq=tq, tk=tk),
        grid=(bsz, DIFF_HEADS, nq),
        in_specs=[
            pl.BlockSpec((tq, hd), lambda b, h, i: (b * nq + i, h)),
            pl.BlockSpec((tq, hd), lambda b, h, i: (b * nq + jnp.minimum(i + 1, nq - 1), h)),
            pl.BlockSpec((seq, hd), lambda b, h, i: (b, DIFF_HEADS + h)),
            pl.BlockSpec((seq, hd), lambda b, h, i: (b, 2 * DIFF_HEADS + h)),
            pl.BlockSpec((None, 2, r, tk, tq), lambda b, h, i: (h, 0, 0, 0, 0)),
            pl.BlockSpec((None, 2, c, c), lambda b, h, i: (h, 0, 0, 0)),
            pl.BlockSpec((None, 2, DIFF_HEAD_DIM), lambda b, h, i: (layer, 0, 0)),
            pl.BlockSpec((None, 2, DIFF_HEAD_DIM), lambda b, h, i: (layer, 0, 0)),
            pl.BlockSpec((None, 1, hd), lambda b, h, i: (layer, 0, 0)),
        ],
        out_specs=pl.BlockSpec((tq, hd), lambda b, h, i: (b * nq + i, h)),
        out_shape=jax.ShapeDtypeStruct((m, DIFF_WIDTH), BF16),
        scratch_shapes=[pltpu.VMEM((2, hd, tq), F32),
                        pltpu.VMEM((2, 1, tq), F32),
                        pltpu.VMEM((2, V7X_SUBLANES, tq), F32),
                        pltpu.VMEM((2, 2, tk, tq), F32),
                        pltpu.VMEM((2, 2, V7X_SUBLANES, tq), F32),
                        pltpu.VMEM((2, DIFF_HEAD_DIM, tq), BF16)],
        compiler_params=_params("parallel", "parallel", "arbitrary"),
        name="diff_attention",
    )(qkv, qkv, qkv, qkv, diag, corner, lam_q, lam_k, subln_g.reshape(-1, 1, hd))


def _mem_attn_kernel(q_ref, kv_ref, o_ref):
    d = MEM_HEAD_DIM
    scale = d ** -0.5
    for h in range(MEM_HEADS):
        q = q_ref[:, h * d:(h + 1) * d]
        k = kv_ref[:, h * d:(h + 1) * d]
        v = kv_ref[:, D_MODEL + h * d:D_MODEL + (h + 1) * d]
        s = lax.dot_general(q, k, (((1,), (1,)), ((), ())), preferred_element_type=F32) * scale
        s = s - jnp.max(s, axis=-1, keepdims=True)
        p = jnp.exp(s)
        p = p / jnp.sum(p, axis=-1, keepdims=True)
        o_ref[:, h * d:(h + 1) * d] = jnp.dot(p.astype(BF16), v,
                                              preferred_element_type=F32).astype(o_ref.dtype)


def _mem_attention(q, kv, seq, tm):
    m = q.shape[0]
    tiles_per_seq = seq // tm
    return pl.pallas_call(
        _mem_attn_kernel,
        grid=(m // tm,),
        in_specs=[pl.BlockSpec((tm, D_MODEL), lambda i: (i, 0)),
                  pl.BlockSpec((N_MEM, 2 * D_MODEL), lambda i: (i // tiles_per_seq, 0))],
        out_specs=pl.BlockSpec((tm, D_MODEL), lambda i: (i, 0)),
        out_shape=jax.ShapeDtypeStruct((m, D_MODEL), BF16),
        compiler_params=_params("parallel"),
        name="mem_attention",
    )(q, kv)


FFN_HALF_ROWS = 512
FFN_RES_FETCH_SLAB = 4


def _ffn_kernel(xb_ref, wg_ref, wu_ref, wd_ref, res_hbm, g_ref, b_ref, o_hbm, ob_hbm,
                acc_ref, rbuf_ref, yb_ref, sems, *, tm):
    i = pl.program_id(0)
    f = pl.program_id(1)
    last_i = pl.num_programs(0) - 1
    last_f = pl.num_programs(1) - 1

    def rows_of(tile):
        return pl.ds(pl.multiple_of(tile * tm, tm), tm)

    def res_copy(tile):
        return pltpu.make_async_copy(res_hbm.at[rows_of(tile), :], rbuf_ref, sems.at[0])

    def out_copies(tile):
        return (pltpu.make_async_copy(rbuf_ref, o_hbm.at[rows_of(tile), :], sems.at[1]),
                pltpu.make_async_copy(yb_ref, ob_hbm.at[rows_of(tile), :], sems.at[2]))

    @pl.when(f == FFN_RES_FETCH_SLAB)
    def _():
        @pl.when(i > 0)
        def _():
            for c in out_copies(i - 1):
                c.wait()

        res_copy(i).start()

    def slab(first):
        wg, wu, wd = (w_ref[...].astype(BF16) for w_ref in (wg_ref, wu_ref, wd_ref))
        for r0 in range(0, tm, FFN_HALF_ROWS):
            rows = slice(r0, r0 + FFN_HALF_ROWS)
            xb = xb_ref[rows, :]
            gate = jnp.dot(xb, wg, preferred_element_type=F32)
            up = jnp.dot(xb, wu, preferred_element_type=F32)
            hid = (gate * (1.0 / (1.0 + jnp.exp(-gate))) * up).astype(BF16)
            part = jnp.dot(hid, wd, preferred_element_type=F32)
            acc_ref[rows, :] = part if first else acc_ref[rows, :] + part

    @pl.when(f == 0)
    def _():
        slab(True)

    @pl.when(f > 0)
    def _():
        slab(False)

    @pl.when(f == last_f)
    def _():
        res_copy(i).wait()
        y = _layer_norm_rows(ALPHA * rbuf_ref[...] + acc_ref[...], g_ref[...], b_ref[...])
        rbuf_ref[...] = y
        yb_ref[...] = y.astype(BF16)
        for c in out_copies(i):
            c.start()

        @pl.when(i == last_i)
        def _():
            for c in out_copies(i):
                c.wait()


def _ffn(xb, w_gu, w_down, layer, res, g, b, tm, tf):
    m, d = xb.shape
    nf = D_FF // tf
    assert FFN_RES_FETCH_SLAB < nf - 1 and tm % FFN_HALF_ROWS == 0
    any_spec = pl.BlockSpec(memory_space=pl.ANY)
    return pl.pallas_call(
        functools.partial(_ffn_kernel, tm=tm),
        grid=(m // tm, nf),
        in_specs=[pl.BlockSpec((tm, d), lambda i, f: (i, 0)),
                  pl.BlockSpec((None, d, tf), lambda i, f: (layer, 0, f)),
                  pl.BlockSpec((None, d, tf), lambda i, f: (layer, 0, nf + f)),
                  pl.BlockSpec((None, tf, d), lambda i, f: (layer, f, 0)),
                  any_spec,
                  pl.BlockSpec((None, 1, d), lambda i, f: (layer, 0, 0)),
                  pl.BlockSpec((None, 1, d), lambda i, f: (layer, 0, 0))],
        out_specs=[any_spec, any_spec],
        out_shape=[jax.ShapeDtypeStruct((m, d), F32),
                   jax.ShapeDtypeStruct((m, d), BF16)],
        scratch_shapes=[pltpu.VMEM((tm, d), F32),
                        pltpu.VMEM((tm, d), F32),
                        pltpu.VMEM((tm, d), BF16),
                        pltpu.SemaphoreType.DMA((3,))],
        compiler_params=_params("arbitrary", "arbitrary"),
        name="ffn",
    )(xb, w_gu, w_gu, w_down, res, g.reshape(-1, 1, d), b.reshape(-1, 1, d))


class _Tiles:
    proj_rows = 2048
    proj_rows_f32 = 1024
    proj_cols_f32 = 1280
    proj_cols = 512
    mem_q_cols = 1024
    mem_kv_cols = 1024
    res_ln_rows = 512
    mixer_rows = 1024
    mem_attn_rows = 2048
    ffn_rows = 1024
    ffn_slab = 256
    attn = 512


def kernel(x, mem, rel_bias, w_in, conv_w, gmlp_ln_g, gmlp_ln_b, gmlp_ws, gmlp_bs,
           diff_lambda_q, diff_lambda_k, diff_subln_g, w_mix_out, ln_mix_g, ln_mix_b,
           w_mem_q, w_mem_kv, w_mem_out, ln_mem_g, ln_mem_b,
           w_ffn_gu, w_ffn_down, ln_ffn_g, ln_ffn_b):
    bsz, seq, d = x.shape
    m = bsz * seq
    x = x.reshape(m, d)
    xb = None
    memb = mem.reshape(bsz * N_MEM, d).astype(BF16)
    t = _Tiles
    bias_tiles = _near_bias_tiles(rel_bias, t.attn, t.attn)
    w_in, w_mix_out, w_mem_q, w_mem_out = (
        w.astype(BF16) for w in (w_in, w_mix_out, w_mem_q, w_mem_out))
    gmlp_bs_t = jnp.transpose(gmlp_bs, (0, 2, 1))
    for l in range(DEPTH):
        lam_init = 0.8 - 0.6 * math.exp(-0.3 * l)
        if xb is None:
            proj_ab, xb = _matmul(x, w_in, l, 0, AB_COLS, F32, t.proj_rows_f32, t.proj_cols_f32)
        else:
            proj_ab = _matmul(xb, w_in, l, 0, AB_COLS, F32, t.proj_rows, t.proj_cols)
        qkv = _matmul(xb, w_in, l, AB_COLS, QKV_COLS, BF16, t.proj_rows, t.proj_cols,
                      scaled_cols=DIFF_WIDTH, scale=ATTN_LOGIT_SCALE)
        y_ab = _mixer_ab(proj_ab, conv_w, gmlp_ln_g, gmlp_ln_b, gmlp_ws, gmlp_bs_t, l, seq,
                         t.mixer_rows)
        y_c = _diff_attention(qkv, bias_tiles, diff_lambda_q, diff_lambda_k, diff_subln_g, l,
                              lam_init, bsz, seq, t.attn, t.attn)
        x, xb = _matmul_res_ln([y_ab, y_c], w_mix_out, l, x, ln_mix_g, ln_mix_b, t.res_ln_rows)

        q_mem = _matmul(xb, w_mem_q, l, 0, D_MODEL, BF16, t.proj_rows, t.mem_q_cols)
        kv_mem = _matmul(memb, w_mem_kv, l, 0, 2 * D_MODEL, BF16, bsz * N_MEM, t.mem_kv_cols)
        o_mem = _mem_attention(q_mem, kv_mem, seq, t.mem_attn_rows)
        x, xb = _matmul_res_ln([o_mem], w_mem_out, l, x, ln_mem_g, ln_mem_b, t.res_ln_rows)

        x, xb = _ffn(xb, w_ffn_gu, w_ffn_down, l, x, ln_ffn_g, ln_ffn_b, t.ffn_rows, t.ffn_slab)
    return x.reshape(bsz, seq, d)
```
